```python
import jax, jax.numpy as jnp
from jax import lax
import numpy as np

D_MODEL = 1024
BATCH = 4
SEQ = 8192
DEPTH = 2
DEC_BATCH = 32
DEC_SEQ = 32
PAST_LEN = 1024

CHUNK = 64
N_EVEN = (DEPTH + 1) // 2
N_ODD = DEPTH // 2
PLE_DIM = 256
NORM_EPS = 1e-6
NEG_INF = -1e30
CONV_W = 4
A_WIDTH = D_MODEL // 2
A_BLOCKS = 8
A_BLK = A_WIDTH // A_BLOCKS
RG_C = 8.0
B_HEADS = 8
B_KV_HEADS = 2
B_HEAD_DIM = 64
B_GROUPS = B_HEADS // B_KV_HEADS
B_Q = B_HEADS * B_HEAD_DIM
B_KV = B_KV_HEADS * B_HEAD_DIM
WINDOW = 128
WIN_CHUNKS = WINDOW // CHUNK
L0_IN = 2 * A_WIDTH + B_Q + 2 * B_KV
L0_MIX = A_WIDTH + B_Q
C_WIDTH = D_MODEL
C_HEADS = 4
C_HEAD_DIM = C_WIDTH // C_HEADS
C_QK_BLK = 4
L1_IN = 3 * C_WIDTH + 2 * C_HEADS
D_FF = ((8 * D_MODEL + 3 * 256 - 1) // (3 * 256)) * 256

kernel_name = 'hybrid_streaming_rglru_swa_mlstm_step'


def _rmsnorm(x, g):
    x32 = x.astype(jnp.float32)
    y = x32 * lax.rsqrt(jnp.mean(x32 * x32, axis=-1, keepdims=True) + NORM_EPS)
    return (y * g.astype(jnp.float32)).astype(x.dtype)


def _causal_conv(u, buf, w, b):
    t = u.shape[1]
    full = jnp.concatenate([buf.astype(u.dtype), u], axis=1)
    out = b + w[0] * full[:, 0:t]
    for j in range(1, CONV_W):
        out = out + w[j] * full[:, j:j + t]
    return out, full[:, t:]


def _block_diag(x, w):
    nb, bs, _ = w.shape
    xb = x.reshape(x.shape[:-1] + (nb, bs))
    return jnp.einsum('ntgi,gij->ntgj', xb, w).reshape(x.shape)


def _rglru(u, h0, w_r, b_r, w_i, b_i, lam):
    u32 = u.astype(jnp.float32)
    r = jax.nn.sigmoid(_block_diag(u32, w_r) + b_r)
    i = jax.nn.sigmoid(_block_diag(u32, w_i) + b_i)
    log_a = -RG_C * r * jax.nn.softplus(-lam.astype(jnp.float32))
    a = jnp.exp(log_a)
    bx = jnp.sqrt(-jnp.expm1(2.0 * log_a)) * (i * u32)
    bx = bx.at[:, 0].add(a[:, 0] * h0.astype(jnp.float32))

    def comb(left, right):
        a1, b1 = left
        a2, b2 = right
        return a1 * a2, a2 * b1 + b2

    _, h = lax.associative_scan(comb, (a, bx), axis=1)
    return h, h[:, -1]


def _alibi_slopes():
    s = 2.0 ** (-8.0 * jnp.arange(1, B_HEADS + 1, dtype=jnp.float32) / B_HEADS)
    return s.reshape(B_KV_HEADS, B_GROUPS)


def _band(a):
    nc = a.shape[1]
    parts = [jnp.pad(a, ((0, 0), (j, 0), (0, 0), (0, 0), (0, 0)))[:, :nc] for j in range(WIN_CHUNKS, -1, -1)]
    return jnp.concatenate(parts, axis=2)


def _band_bias(nc, slopes):
    c = jnp.arange(nc)[:, None]
    qpos = c * CHUNK + jnp.arange(CHUNK)[None, :]
    back = jnp.repeat(jnp.arange(WIN_CHUNKS, -1, -1), CHUNK)
    kchunk = c - back[None, :]
    kpos = kchunk * CHUNK + jnp.tile(jnp.arange(CHUNK), WIN_CHUNKS + 1)[None, :]
    dist = jnp.abs(qpos[:, :, None] - kpos[:, None, :]).astype(jnp.float32)
    bias = -slopes[None, :, :, None, None] * dist[:, None, None, :, :]
    return jnp.where((kchunk >= 0)[:, None, None, None, :], bias, NEG_INF)


def _attend(q, k, v, bias, sinks):
    s = jnp.einsum('...qkgd,...skd->...kgqs', q.astype(jnp.float32), k.astype(jnp.float32))
    s = s * (B_HEAD_DIM ** -0.5) + bias
    sink = jnp.broadcast_to(sinks.astype(jnp.float32)[:, :, None, None], s.shape[:-1] + (1,))
    p = jax.nn.softmax(jnp.concatenate([s, sink], axis=-1), axis=-1)[..., :-1]
    return jnp.einsum('...kgqs,...skd->...qkgd', p, v.astype(jnp.float32))


def _mix_ab(xn, conv0, h0, cache_k, cache_v, w_in, conv_w, conv_b, w_r, b_r, w_i, b_i, lam, sinks, w_out):
    n_b, t, _ = xn.shape
    proj = xn @ w_in
    g, u, q, k, v = jnp.split(proj, [A_WIDTH, 2 * A_WIDTH, 2 * A_WIDTH + B_Q, 2 * A_WIDTH + B_Q + B_KV], axis=-1)
    uc, new_conv = _causal_conv(u, conv0, conv_w, conv_b)
    hs, h_last = _rglru(uc, h0, w_r, b_r, w_i, b_i, lam)
    y_a = (hs * jax.nn.gelu(g.astype(jnp.float32))).astype(xn.dtype)
    qh = q.reshape(n_b, t, B_KV_HEADS, B_GROUPS, B_HEAD_DIM)
    kh = k.reshape(n_b, t, B_KV_HEADS, B_HEAD_DIM)
    vh = v.reshape(n_b, t, B_KV_HEADS, B_HEAD_DIM)
    slopes = _alibi_slopes()
    sk = sinks.reshape(B_KV_HEADS, B_GROUPS)
    if cache_k is None:
        nc = t // CHUNK
        qb = qh.reshape(n_b, nc, CHUNK, B_KV_HEADS, B_GROUPS, B_HEAD_DIM)
        kb = _band(kh.reshape(n_b, nc, CHUNK, B_KV_HEADS, B_HEAD_DIM))
        vb = _band(vh.reshape(n_b, nc, CHUNK, B_KV_HEADS, B_HEAD_DIM))
        o = _attend(qb, kb, vb, _band_bias(nc, slopes), sk).reshape(n_b, t, B_Q)
        rows = min(WINDOW, t)
        new_k, new_v = kh[:, t - rows:], vh[:, t - rows:]
    else:
        w_rows = cache_k.shape[1]
        kf = jnp.concatenate([cache_k.astype(kh.dtype), kh], axis=1)
        vf = jnp.concatenate([cache_v.astype(vh.dtype), vh], axis=1)
        qpos = PAST_LEN + jnp.arange(t)
        kpos = PAST_LEN - w_rows + jnp.arange(w_rows + t)
        dist = jnp.abs(qpos[:, None] - kpos[None, :]).astype(jnp.float32)
        bias = -slopes[:, :, None, None] * dist[None, None]
        o = _attend(qh, kf, vf, bias, sk).reshape(n_b, t, B_Q)
        new_k, new_v = kh, vh
    y = jnp.concatenate([y_a, o.astype(xn.dtype)], axis=-1) @ w_out
    return y, new_conv, h_last, new_k, new_v


def _mlstm_chunk(carry, xs):
    c_m, n_m, m = carry
    q, k, v, ig, lf = xs
    l = q.shape[1]
    f_cum = jnp.cumsum(lf, axis=1).transpose(0, 2, 1)
    igh = ig.transpose(0, 2, 1)
    causal = jnp.tril(jnp.ones((l, l), dtype=bool))
    dlog = jnp.where(causal, f_cum[..., :, None] - f_cum[..., None, :] + igh[..., None, :], -jnp.inf)
    inter = f_cum + m[..., None]
    m_t = jnp.maximum(inter, dlog.max(axis=-1))
    dw = jnp.exp(dlog - m_t[..., None])
    s = jnp.einsum('nthd,nshd->nhts', q, k) * dw
    w_prev = jnp.exp(inter - m_t)
    num = jnp.einsum('nhts,nshd->nthd', s, v) + w_prev.transpose(0, 2, 1)[..., None] * jnp.einsum('nthd,nhde->nthe', q, c_m)
    den = s.sum(axis=-1) + w_prev * jnp.einsum('nthd,nhd->nht', q, n_m)
    norm = jnp.maximum(jnp.abs(den), jnp.exp(-m_t))
    h = num / norm.transpose(0, 2, 1)[..., None]
    f_last = f_cum[..., -1]
    wlog = f_last[..., None] - f_cum + igh
    m_new = jnp.maximum(f_last + m, wlog.max(axis=-1))
    ws = jnp.exp(wlog - m_new[..., None])
    decay = jnp.exp(f_last + m - m_new)
    c_new = decay[..., None, None] * c_m + jnp.einsum('nhs,nshd,nshe->nhde', ws, k, v)
    n_new = decay[..., None] * n_m + jnp.einsum('nhs,nshd->nhd', ws, k)
    return (c_new, n_new, m_new), h


def _mix_c(xn, conv0, c0, n0, m0, w_in, b_gate, conv_w, conv_b, w_q, w_k, norm_g, w_out):
    n_b, t, _ = xn.shape
    proj = xn @ w_in
    u, v, o, gates = jnp.split(proj, [C_WIDTH, 2 * C_WIDTH, 3 * C_WIDTH], axis=-1)
    gates = gates.astype(jnp.float32) + b_gate.astype(jnp.float32)
    ig = gates[..., :C_HEADS]
    lf = jax.nn.log_sigmoid(gates[..., C_HEADS:])
    uc, new_conv = _causal_conv(u, conv0, conv_w, conv_b)
    uc = jax.nn.silu(uc.astype(jnp.float32))
    hshape = (n_b, t, C_HEADS, C_HEAD_DIM)
    q = _block_diag(uc, w_q).reshape(hshape)
    k = _block_diag(uc, w_k).reshape(hshape) * (C_HEAD_DIM ** -0.5)
    vv = v.astype(jnp.float32).reshape(hshape)
    l = min(t, CHUNK)
    nc = t // l

    def blocks(a):
        return jnp.moveaxis(a.reshape((n_b, nc, l) + a.shape[2:]), 1, 0)

    state0 = (c0.astype(jnp.float32), n0.astype(jnp.float32), m0.astype(jnp.float32))
    (c1, n1, m1), hs = lax.scan(_mlstm_chunk, state0, (blocks(q), blocks(k), blocks(vv), blocks(ig), blocks(lf)))
    h = jnp.moveaxis(hs, 0, 1).reshape(hshape)
    h = jax.nn.sigmoid(o.astype(jnp.float32)).reshape(hshape) * h
    mu = jnp.mean(h, axis=-1, keepdims=True)
    var = jnp.mean(jnp.square(h - mu), axis=-1, keepdims=True)
    h = ((h - mu) * lax.rsqrt(var + NORM_EPS)).reshape(n_b, t, C_WIDTH) * norm_g.astype(jnp.float32)
    y = h.astype(xn.dtype) @ w_out
    return y, new_conv, c1, n1, m1


def _swiglu(x, w_in, w_out):
    gate, up = jnp.split(x @ w_in, 2, axis=-1)
    return (jax.nn.silu(gate) * up) @ w_out


def _trunk(x, p, cache, w):
    n_b = x.shape[0]
    h = x
    a_conv, a_h, b_k, b_v, c_conv, c_c, c_n, c_m = [], [], [], [], [], [], [], []
    for li in range(DEPTH):
        j = li // 2
        hn = _rmsnorm(h, w['norm_mix_g'][li])
        if li % 2 == 0:
            if cache is None:
                conv0 = jnp.zeros((n_b, CONV_W - 1, A_WIDTH), x.dtype)
                h0 = jnp.zeros((n_b, A_WIDTH), jnp.float32)
                ck, cv = None, None
            else:
                conv0, h0 = cache['a_conv'][j], cache['a_h'][j]
                ck, cv = cache['b_k'][j], cache['b_v'][j]
            y, s_conv, s_h, s_k, s_v = _mix_ab(hn, conv0, h0, ck, cv, w['ab_w_in'][j], w['ab_conv_w'][j], w['ab_conv_b'][j], w['ab_w_r'][j], w['ab_b_r'][j], w['ab_w_i'][j], w['ab_b_i'][j], w['ab_lambda'][j], w['ab_sinks'][j], w['ab_w_out'][j])
            a_conv.append(s_conv)
            a_h.append(s_h)
            b_k.append(s_k)
            b_v.append(s_v)
        else:
            if cache is None:
                conv0 = jnp.zeros((n_b, CONV_W - 1, C_WIDTH), x.dtype)
                c0 = jnp.zeros((n_b, C_HEADS, C_HEAD_DIM, C_HEAD_DIM), jnp.float32)
                n0 = jnp.zeros((n_b, C_HEADS, C_HEAD_DIM), jnp.float32)
                m0 = jnp.zeros((n_b, C_HEADS), jnp.float32)
            else:
                conv0, c0, n0, m0 = cache['c_conv'][j], cache['c_C'][j], cache['c_n'][j], cache['c_m'][j]
            y, s_conv, s_c, s_n, s_m = _mix_c(hn, conv0, c0, n0, m0, w['c_w_in'][j], w['c_b_gate'][j], w['c_conv_w'][j], w['c_conv_b'][j], w['c_w_q'][j], w['c_w_k'][j], w['c_norm_g'][j], w['c_w_out'][j])
            c_conv.append(s_conv)
            c_c.append(s_c)
            c_n.append(s_n)
            c_m.append(s_m)
        h = h + y
        h = h + _swiglu(_rmsnorm(h, w['norm_ffn_g'][li]), w['ffn_w_in'][li], w['ffn_w_out'][li])
        h = h + jax.nn.sigmoid(h @ w['ple_w_gate'][li]) * (p[li] @ w['ple_w_proj'][li])
    y_out = _rmsnorm(h, w['final_g'])
    return (y_out, jnp.stack(a_conv), jnp.stack(a_h), jnp.stack(b_k), jnp.stack(b_v), jnp.stack(c_conv), jnp.stack(c_c), jnp.stack(c_n), jnp.stack(c_m))


def setup_inputs(seed: int = 0) -> dict:
    key = jax.random.key(seed)
    ks = iter(jax.random.split(key, 48))

    def nrm(shape, scale):
        return jax.random.normal(next(ks), shape, jnp.float32) * scale

    win_rows = min(WINDOW, PAST_LEN)
    u = jax.random.uniform(next(ks), (N_EVEN, A_WIDTH), jnp.float32, minval=0.9, maxval=0.999)
    s = u ** (1.0 / RG_C)
    lam = jnp.log(s) - jnp.log1p(-s)
    b_gate = jnp.concatenate([nrm((N_ODD, C_HEADS), 0.1), jnp.linspace(3.0, 6.0, C_HEADS)[None, :] + nrm((N_ODD, C_HEADS), 0.1)], axis=-1)
    return {
        'x_prompt': nrm((BATCH, SEQ, D_MODEL), 1.0),
        'x_sample': nrm((DEC_BATCH, DEC_SEQ, D_MODEL), 1.0),
        'cache_a_conv': nrm((N_EVEN, DEC_BATCH, CONV_W - 1, A_WIDTH), 1.0),
        'state_a_h': nrm((N_EVEN, DEC_BATCH, A_WIDTH), 0.5),
        'cache_b_k': nrm((N_EVEN, DEC_BATCH, win_rows, B_KV_HEADS, B_HEAD_DIM), 1.0),
        'cache_b_v': nrm((N_EVEN, DEC_BATCH, win_rows, B_KV_HEADS, B_HEAD_DIM), 1.0),
        'cache_c_conv': nrm((N_ODD, DEC_BATCH, CONV_W - 1, C_WIDTH), 1.0),
        'state_c_C': nrm((N_ODD, DEC_BATCH, C_HEADS, C_HEAD_DIM, C_HEAD_DIM), 0.1),
        'state_c_n': nrm((N_ODD, DEC_BATCH, C_HEADS, C_HEAD_DIM), 0.1),
        'state_c_m': nrm((N_ODD, DEC_BATCH, C_HEADS), 0.5),
        'p_prompt': nrm((DEPTH, BATCH, SEQ, PLE_DIM), 1.0),
        'p_sample': nrm((DEPTH, DEC_BATCH, DEC_SEQ, PLE_DIM), 1.0),
        'norm_mix_g': 1.0 + nrm((DEPTH, D_MODEL), 0.01),
        'norm_ffn_g': 1.0 + nrm((DEPTH, D_MODEL), 0.01),
        'final_g': 1.0 + nrm((D_MODEL,), 0.01),
        'ab_w_in': nrm((N_EVEN, D_MODEL, L0_IN), D_MODEL ** -0.5),
        'ab_conv_w': nrm((N_EVEN, CONV_W, A_WIDTH), CONV_W ** -0.5),
        'ab_conv_b': nrm((N_EVEN, A_WIDTH), 0.01),
        'ab_w_r': nrm((N_EVEN, A_BLOCKS, A_BLK, A_BLK), A_BLK ** -0.5),
        'ab_b_r': nrm((N_EVEN, A_WIDTH), 0.1),
        'ab_w_i': nrm((N_EVEN, A_BLOCKS, A_BLK, A_BLK), A_BLK ** -0.5),
        'ab_b_i': nrm((N_EVEN, A_WIDTH), 0.1),
        'ab_lambda': lam,
        'ab_sinks': nrm((N_EVEN, B_HEADS), 0.5),
        'ab_w_out': nrm((N_EVEN, L0_MIX, D_MODEL), L0_MIX ** -0.5),
        'c_w_in': nrm((N_ODD, D_MODEL, L1_IN), D_MODEL ** -0.5),
        'c_b_gate': b_gate,
        'c_conv_w': nrm((N_ODD, CONV_W, C_WIDTH), CONV_W ** -0.5),
        'c_conv_b': nrm((N_ODD, C_WIDTH), 0.01),
        'c_w_q': nrm((N_ODD, C_WIDTH // C_QK_BLK, C_QK_BLK, C_QK_BLK), C_QK_BLK ** -0.5),
        'c_w_k': nrm((N_ODD, C_WIDTH // C_QK_BLK, C_QK_BLK, C_QK_BLK), C_QK_BLK ** -0.5),
        'c_norm_g': 1.0 + nrm((N_ODD, C_WIDTH), 0.01),
        'c_w_out': nrm((N_ODD, C_WIDTH, D_MODEL), C_WIDTH ** -0.5),
        'ffn_w_in': nrm((DEPTH, D_MODEL, 2 * D_FF), D_MODEL ** -0.5),
        'ffn_w_out': nrm((DEPTH, D_FF, D_MODEL), D_FF ** -0.5),
        'ple_w_proj': nrm((DEPTH, PLE_DIM, D_MODEL), PLE_DIM ** -0.5),
        'ple_w_gate': nrm((DEPTH, D_MODEL, D_MODEL), D_MODEL ** -0.5),
    }


def reference(x_prompt, x_sample, cache_a_conv, state_a_h, cache_b_k, cache_b_v, cache_c_conv, state_c_C, state_c_n, state_c_m, p_prompt, p_sample, norm_mix_g, norm_ffn_g, final_g, ab_w_in, ab_conv_w, ab_conv_b, ab_w_r, ab_b_r, ab_w_i, ab_b_i, ab_lambda, ab_sinks, ab_w_out, c_w_in, c_b_gate, c_conv_w, c_conv_b, c_w_q, c_w_k, c_norm_g, c_w_out, ffn_w_in, ffn_w_out, ple_w_proj, ple_w_gate):
    w = dict(norm_mix_g=norm_mix_g, norm_ffn_g=norm_ffn_g, final_g=final_g, ab_w_in=ab_w_in, ab_conv_w=ab_conv_w, ab_conv_b=ab_conv_b, ab_w_r=ab_w_r, ab_b_r=ab_b_r, ab_w_i=ab_w_i, ab_b_i=ab_b_i, ab_lambda=ab_lambda, ab_sinks=ab_sinks, ab_w_out=ab_w_out, c_w_in=c_w_in, c_b_gate=c_b_gate, c_conv_w=c_conv_w, c_conv_b=c_conv_b, c_w_q=c_w_q, c_w_k=c_w_k, c_norm_g=c_norm_g, c_w_out=c_w_out, ffn_w_in=ffn_w_in, ffn_w_out=ffn_w_out, ple_w_proj=ple_w_proj, ple_w_gate=ple_w_gate)
    (y_prompt, pa_conv, pa_h, pb_k, pb_v, pc_conv, pc_C, pc_n, pc_m) = _trunk(x_prompt, p_prompt, None, w)
    cache = dict(a_conv=cache_a_conv, a_h=state_a_h, b_k=cache_b_k, b_v=cache_b_v, c_conv=cache_c_conv, c_C=state_c_C, c_n=state_c_n, c_m=state_c_m)
    (y_sample, sa_conv, sa_h, sb_k, sb_v, sc_conv, sc_C, sc_n, sc_m) = _trunk(x_sample, p_sample, cache, w)
    return (y_prompt, y_sample, pa_conv, pa_h, pb_k, pb_v, pc_conv, pc_C, pc_n, pc_m, sa_conv, sa_h, sb_k, sb_v, sc_conv, sc_C, sc_n, sc_m)
```

```python
import functools

import jax
import jax.numpy as jnp
from jax import lax
from jax.experimental import pallas as pl
from jax.experimental.pallas import tpu as pltpu

F32 = jnp.float32
BF16 = jnp.bfloat16

CHUNK = 64
NORM_EPS = 1e-6
NEG_INF = -1e30
CONV_W = 4
RG_C = 8.0
A_BLOCKS = 8
B_HEADS = 8
B_KV_HEADS = 2
B_HEAD_DIM = 64
B_GROUPS = B_HEADS // B_KV_HEADS
WINDOW = 128
C_HEADS = 4
C_QK_BLK = 4

LANES = 128
SUBLANES = 8
VMEM_LIMIT_BYTES = 56 * 1024 * 1024

_NT = (((1,), (1,)), ((), ()))
_TN = (((0,), (0,)), ((), ()))


def _params(n_grid_axes):
    return pltpu.CompilerParams(
        dimension_semantics=("arbitrary",) * n_grid_axes,
        vmem_limit_bytes=VMEM_LIMIT_BYTES)


def _resident(shape):
    nd = len(shape)
    return pl.BlockSpec(shape, lambda *_: (0,) * nd, pipeline_mode=pl.Buffered(1))


def _rms(x, g):
    return x * lax.rsqrt(jnp.mean(x * x, axis=-1, keepdims=True) + NORM_EPS) * g


def _log_sigmoid(x):
    return jnp.minimum(x, 0.0) - jnp.log1p(jnp.exp(-jnp.abs(x)))


def _softplus(x):
    return jnp.maximum(x, 0.0) + jnp.log1p(jnp.exp(-jnp.abs(x)))


def _split3(x):
    hi = x.astype(BF16)
    r1 = x - hi.astype(F32)
    mid = r1.astype(BF16)
    lo = (r1 - mid.astype(F32)).astype(BF16)
    return hi, mid, lo


def _rms_inproj_body(x_ref, g_ref, w_ref, *out_refs):
    xn = _rms(x_ref[...], g_ref[...]).astype(BF16)
    off = 0
    for o_ref in out_refs:
        wd = o_ref.shape[-1]
        o_ref[...] = jnp.dot(xn, w_ref[:, off:off + wd], preferred_element_type=F32)
        off += wd


def _rms_inproj(x, g, w, widths, tm, name):
    n, d = x.shape
    assert n % tm == 0 and sum(widths) == w.shape[1]
    return pl.pallas_call(
        _rms_inproj_body,
        grid=(n // tm,),
        in_specs=[pl.BlockSpec((tm, d), lambda i: (i, 0)), _resident((1, d)), _resident(w.shape)],
        out_specs=[pl.BlockSpec((tm, wd), lambda i: (i, 0)) for wd in widths],
        out_shape=[jax.ShapeDtypeStruct((n, wd), F32) for wd in widths],
        compiler_params=_params(1),
        name=name,
    )(x, g, w)


def _conv_tile(u, ext_ref, cw_ref, cb_ref, rows):
    ext_ref[SUBLANES:SUBLANES + rows, :] = u
    cw = cw_ref[...]
    out = cb_ref[...] + cw[0:1] * ext_ref[5:5 + rows, :]
    out = out + cw[1:2] * ext_ref[6:6 + rows, :]
    out = out + cw[2:3] * ext_ref[7:7 + rows, :]
    out = out + cw[3:4] * u
    ext_ref[0:SUBLANES, :] = ext_ref[rows:rows + SUBLANES, :]
    return out


def _rglru_body(u_ref, g_ref, conv0_ref, h0_ref, cw_ref, cb_ref, wri_ref, br_ref, bi_ref, lam_ref,
                ya_ref, hlast_ref, ext_ref, a_s, b_s, h_s, carry_ref, *, rows):
    t = pl.program_id(1)

    @pl.when(t == 0)
    def _init():
        ext_ref[0:SUBLANES, :] = conv0_ref[0]
        carry_ref[...] = h0_ref[0]

    uc = _conv_tile(u_ref[...], ext_ref, cw_ref, cb_ref, rows)
    width = uc.shape[1]
    ucb = uc.astype(BF16)
    r_parts, i_parts = [], []
    for kt in range(width // LANES):
        pre = jnp.dot(ucb[:, kt * LANES:(kt + 1) * LANES], wri_ref[kt], preferred_element_type=F32)
        r_parts.append(pre[:, :LANES])
        i_parts.append(pre[:, LANES:])
    r = jax.nn.sigmoid(jnp.concatenate(r_parts, axis=1) + br_ref[...])
    ig = jax.nn.sigmoid(jnp.concatenate(i_parts, axis=1) + bi_ref[...])
    log_a = -RG_C * r * _softplus(-lam_ref[...])
    a = jnp.exp(log_a)
    one_minus_a2 = -jnp.tanh(log_a) * (a * a + 1.0)
    bx = jnp.sqrt(one_minus_a2) * (ig * uc)

    nb = rows // SUBLANES
    a3 = a.reshape(nb, SUBLANES, width)
    b3 = bx.reshape(nb, SUBLANES, width)
    row = lax.broadcasted_iota(jnp.int32, a3.shape, 1)
    for s in (1, 2, 4):
        keep = row >= s
        a_sh = pltpu.roll(a3, s, axis=1)
        b_sh = pltpu.roll(b3, s, axis=1)
        b3 = jnp.where(keep, a3 * b_sh + b3, b3)
        a3 = jnp.where(keep, a3 * a_sh, a3)
    a_s[...] = a3
    b_s[...] = b3

    def group(i, carry):
        hb = a_s[i] * carry + b_s[i]
        h_s[i] = hb
        return hb[SUBLANES - 1:SUBLANES, :]

    carry = lax.fori_loop(0, nb, group, carry_ref[...], unroll=min(nb, 8))
    carry_ref[...] = carry
    hlast_ref[0] = carry
    h = h_s[...].reshape(rows, width)
    ya_ref[...] = h * jax.nn.gelu(g_ref[...])


def _rglru(u, g, conv0, h0, cw, cb, wri, br, bi, lam, n_seq, rows, name):
    n, width = u.shape
    nt = n // (n_seq * rows)
    assert nt * n_seq * rows == n and rows % SUBLANES == 0
    tok = pl.BlockSpec((rows, width), lambda s, t: (s * nt + t, 0))
    return pl.pallas_call(
        functools.partial(_rglru_body, rows=rows),
        grid=(n_seq, nt),
        in_specs=[tok, tok,
                  pl.BlockSpec((1, SUBLANES, width), lambda s, t: (s, 0, 0)),
                  pl.BlockSpec((1, 1, width), lambda s, t: (s, 0, 0)),
                  _resident(cw.shape), _resident(cb.shape), _resident(wri.shape),
                  _resident(br.shape), _resident(bi.shape), _resident(lam.shape)],
        out_specs=[tok, pl.BlockSpec((1, 1, width), lambda s, t: (s, 0, 0))],
        out_shape=[jax.ShapeDtypeStruct((n, width), F32),
                   jax.ShapeDtypeStruct((n_seq, 1, width), F32)],
        scratch_shapes=[pltpu.VMEM((rows + SUBLANES, width), F32),
                        pltpu.VMEM((rows // SUBLANES, SUBLANES, width), F32),
                        pltpu.VMEM((rows // SUBLANES, SUBLANES, width), F32),
                        pltpu.VMEM((rows // SUBLANES, SUBLANES, width), F32),
                        pltpu.VMEM((1, width), F32)],
        compiler_params=_params(2),
        name=name,
    )(u, g, conv0, h0, cw, cb, wri, br, bi, lam)


def _attn_core(q, kf, vf, sinks_ref, key_ok):
    n_q, n_k = q.shape[0], kf.shape[0]
    ri = lax.broadcasted_iota(jnp.int32, (n_q, n_k), 0)
    ci = lax.broadcasted_iota(jnp.int32, (n_q, n_k), 1)
    dist = jnp.abs(ri + WINDOW - ci).astype(F32)
    qb, kb, vb = q.astype(BF16), kf.astype(BF16), vf.astype(BF16)
    outs = []
    for h in range(B_HEADS):
        kv = h // B_GROUPS
        hs = slice(h * B_HEAD_DIM, (h + 1) * B_HEAD_DIM)
        kvs = slice(kv * B_HEAD_DIM, (kv + 1) * B_HEAD_DIM)
        bias = (-(2.0 ** (-8.0 * (h + 1) / B_HEADS))) * dist
        if key_ok is not None:
            bias = jnp.where(key_ok, bias, NEG_INF)
        s = lax.dot_general(qb[:, hs], kb[:, kvs], _NT, preferred_element_type=F32)
        s = s * (B_HEAD_DIM ** -0.5) + bias
        sink = sinks_ref[h]
        m = jnp.maximum(jnp.max(s, axis=-1, keepdims=True), sink)
        p = jnp.exp(s - m)
        den = jnp.sum(p, axis=-1, keepdims=True) + jnp.exp(sink - m)
        o = jnp.dot(p.astype(BF16), vb[:, kvs], preferred_element_type=F32)
        outs.append(o / den)
    return jnp.concatenate(outs, axis=1)


def _attn_prompt_body(sinks_ref, q_ref, k_ref, kh_ref, v_ref, vh_ref, o_ref, *, rows):
    t = pl.program_id(1)
    kf = jnp.concatenate([kh_ref[...], k_ref[...]], axis=0)
    vf = jnp.concatenate([vh_ref[...], v_ref[...]], axis=0)
    sc_rows = 2 * CHUNK
    n_k = sc_rows + WINDOW
    ri = lax.broadcasted_iota(jnp.int32, (sc_rows, n_k), 0)
    ci = lax.broadcasted_iota(jnp.int32, (sc_rows, n_k), 1)
    qc = ri // CHUNK
    jc = ci // CHUNK
    in_band = (jc >= qc) & (jc <= qc + WINDOW // CHUNK)
    for sc in range(rows // sc_rows):
        c0 = t * (rows // CHUNK) + sc * (sc_rows // CHUNK)
        key_ok = in_band & (jc >= WINDOW // CHUNK - c0)
        r0 = sc * sc_rows
        o_ref[r0:r0 + sc_rows, :] = _attn_core(
            q_ref[r0:r0 + sc_rows, :], kf[r0:r0 + n_k], vf[r0:r0 + n_k], sinks_ref, key_ok)


def _attn_prompt(q, k, v, sinks, n_seq, rows, name):
    n, qw = q.shape
    kw = k.shape[1]
    nt = n // (n_seq * rows)
    assert rows % (2 * CHUNK) == 0 and nt * n_seq * rows == n
    per = rows // WINDOW
    main = lambda w: pl.BlockSpec((rows, w), lambda s, t: (s * nt + t, 0))
    halo = pl.BlockSpec((WINDOW, kw), lambda s, t: (jnp.maximum((s * nt + t) * per - 1, 0), 0))
    return pl.pallas_call(
        functools.partial(_attn_prompt_body, rows=rows),
        grid=(n_seq, nt),
        in_specs=[pl.BlockSpec(memory_space=pltpu.SMEM), main(qw), main(kw), halo, main(kw), halo],
        out_specs=main(qw),
        out_shape=jax.ShapeDtypeStruct((n, qw), F32),
        compiler_params=_params(2),
        name=name,
    )(sinks, q, k, k, v, v)


def _attn_sample_body(sinks_ref, q_ref, k_ref, ck_ref, v_ref, cv_ref, o_ref):
    kf = jnp.concatenate([ck_ref[0], k_ref[...]], axis=0)
    vf = jnp.concatenate([cv_ref[0], v_ref[...]], axis=0)
    o_ref[...] = _attn_core(q_ref[...], kf, vf, sinks_ref, None)


def _attn_sample(q, k, v, cache_k, cache_v, sinks, n_seq, name):
    n, qw = q.shape
    kw = k.shape[1]
    rows = n // n_seq
    assert cache_k.shape == (n_seq, WINDOW, kw)
    tok = lambda w: pl.BlockSpec((rows, w), lambda s: (s, 0))
    cache = pl.BlockSpec((1, WINDOW, kw), lambda s: (s, 0, 0))
    return pl.pallas_call(
        _attn_sample_body,
        grid=(n_seq,),
        in_specs=[pl.BlockSpec(memory_space=pltpu.SMEM), tok(qw), tok(kw), cache, tok(kw), cache],
        out_specs=tok(qw),
        out_shape=jax.ShapeDtypeStruct((n, qw), F32),
        compiler_params=_params(1),
        name=name,
    )(sinks, q, k, cache_k, v, cache_v)


def _mlstm_body(u_ref, v_ref, o_ref, g_ref, gt_ref, conv0_ref, c0_ref, n0_ref, m0_ref,
                cw_ref, cb_ref, wqk_ref, bg_ref, bgt_ref, ng_ref,
                hn_ref, c_out_ref, n_out_ref, m_out_ref,
                ext_ref, c_s, n_s, m_s, *, rows):
    t = pl.program_id(1)

    @pl.when(t == 0)
    def _init():
        ext_ref[0:SUBLANES, :] = conv0_ref[0]
        c_s[...] = c0_ref[0]
        n_s[...] = n0_ref[0]
        m_s[...] = m0_ref[0]

    uc = _conv_tile(u_ref[...], ext_ref, cw_ref, cb_ref, rows)
    uc = uc * jax.nn.sigmoid(uc)
    width = uc.shape[1]
    dh = width // C_HEADS
    ucb = uc.astype(BF16)
    q_parts, k_parts = [], []
    for kt in range(width // LANES):
        qk = jnp.dot(ucb[:, kt * LANES:(kt + 1) * LANES], wqk_ref[kt], preferred_element_type=F32)
        q_parts.append(qk[:, :LANES])
        k_parts.append(qk[:, LANES:])
    q = jnp.concatenate(q_parts, axis=1)
    k = jnp.concatenate(k_parts, axis=1)

    gcol = g_ref[...] + bg_ref[...]
    grow = gt_ref[0] + bgt_ref[...]
    ri = lax.broadcasted_iota(jnp.int32, (rows, rows), 0)
    ci = lax.broadcasted_iota(jnp.int32, (rows, rows), 1)
    causal = ci <= ri
    tril = causal.astype(BF16)
    triu = (ri <= ci).astype(BF16)
    fc_col = sum(jnp.dot(tril, part, preferred_element_type=F32) for part in _split3(_log_sigmoid(gcol)))
    fc_row = sum(jnp.dot(part, triu, preferred_element_type=F32) for part in _split3(_log_sigmoid(grow)))

    for h in range(C_HEADS):
        hs = slice(h * dh, (h + 1) * dh)
        f_col = fc_col[:, C_HEADS + h:C_HEADS + h + 1]
        i_col = gcol[:, h:h + 1]
        f_row = fc_row[C_HEADS + h:C_HEADS + h + 1, :]
        i_row = grow[h:h + 1, :]
        m_prev = m_s[h:h + 1, 0:1]
        dlog = jnp.where(causal, f_col - f_row + i_row, -jnp.inf)
        inter = f_col + m_prev
        m_t = jnp.maximum(inter, jnp.max(dlog, axis=-1, keepdims=True))
        dw = jnp.exp(dlog - m_t)
        qh, kh = q[:, hs], k[:, hs]
        qb, vb = qh.astype(BF16), v_ref[:, hs].astype(BF16)
        s = lax.dot_general(qb, kh.astype(BF16), _NT, preferred_element_type=F32) * dw
        w_prev = jnp.exp(inter - m_t)
        c_prev = c_s[h]
        n_prev = n_s[h:h + 1, :]
        num = (jnp.dot(s.astype(BF16), vb, preferred_element_type=F32)
               + w_prev * jnp.dot(qb, c_prev.astype(BF16), preferred_element_type=F32))
        den = (jnp.sum(s, axis=-1, keepdims=True)
               + w_prev * jnp.sum(qh * n_prev, axis=-1, keepdims=True))
        hh = num / jnp.maximum(jnp.abs(den), jnp.exp(-m_t))

        f_last = f_col[rows - 1:rows, :]
        wlog = f_last - f_col + i_col
        m_new = jnp.maximum(f_last + m_prev, jnp.max(wlog, axis=0, keepdims=True))
        kw = jnp.exp(wlog - m_new) * kh
        decay = jnp.exp(f_last + m_prev - m_new)
        c_s[h] = decay * c_prev + lax.dot_general(kw.astype(BF16), vb, _TN, preferred_element_type=F32)
        n_s[h:h + 1, :] = decay * n_prev + jnp.sum(kw, axis=0, keepdims=True)
        m_s[h:h + 1, :] = jnp.broadcast_to(m_new, (1, LANES))

        hg = jax.nn.sigmoid(o_ref[:, hs]) * hh
        mu = jnp.mean(hg, axis=-1, keepdims=True)
        var = jnp.mean(jnp.square(hg - mu), axis=-1, keepdims=True)
        hn_ref[:, hs] = (hg - mu) * lax.rsqrt(var + NORM_EPS) * ng_ref[:, hs]

    @pl.when(t == pl.num_programs(1) - 1)
    def _finish():
        c_out_ref[0] = c_s[...]
        n_out_ref[0] = n_s[...]
        m_out_ref[0] = m_s[...]


def _mlstm(u, v, o, gates, gates_t, conv0, c0, n0, m0, cw, cb, wqk, bg, bgt, ng, n_seq, rows, name):
    n, width = u.shape
    nt = n // (n_seq * rows)
    dh = width // C_HEADS
    assert nt * n_seq * rows == n and gates_t.shape == (n_seq * nt, SUBLANES, rows)
    tok = lambda w: pl.BlockSpec((rows, w), lambda s, t: (s * nt + t, 0))
    per_seq = lambda shape: pl.BlockSpec((1,) + shape, lambda s, t: (s,) + (0,) * len(shape))
    return pl.pallas_call(
        functools.partial(_mlstm_body, rows=rows),
        grid=(n_seq, nt),
        in_specs=[tok(width), tok(width), tok(width), tok(LANES),
                  pl.BlockSpec((1, SUBLANES, rows), lambda s, t: (s * nt + t, 0, 0)),
                  per_seq((SUBLANES, width)), per_seq((C_HEADS, dh, dh)),
                  per_seq((C_HEADS, dh)), per_seq((C_HEADS, LANES)),
                  _resident(cw.shape), _resident(cb.shape), _resident(wqk.shape),
                  _resident(bg.shape), _resident(bgt.shape), _resident(ng.shape)],
        out_specs=[tok(width), per_seq((C_HEADS, dh, dh)), per_seq((C_HEADS, dh)), per_seq((C_HEADS, LANES))],
        out_shape=[jax.ShapeDtypeStruct((n, width), F32),
                   jax.ShapeDtypeStruct((n_seq, C_HEADS, dh, dh), F32),
                   jax.ShapeDtypeStruct((n_seq, C_HEADS, dh), F32),
                   jax.ShapeDtypeStruct((n_seq, C_HEADS, LANES), F32)],
        scratch_shapes=[pltpu.VMEM((rows + SUBLANES, width), F32),
                        pltpu.VMEM((C_HEADS, dh, dh), F32),
                        pltpu.VMEM((C_HEADS, dh), F32),
                        pltpu.VMEM((C_HEADS, LANES), F32)],
        compiler_params=_params(2),
        name=name,
    )(u, v, o, gates, gates_t, conv0, c0, n0, m0, cw, cb, wqk, bg, bgt, ng)


FF_CHUNK = 256


def _tail_body(*refs, n_mix, final):
    x_ref = refs[0]
    mix_refs = refs[1:1 + n_mix]
    wmix_refs = refs[1 + n_mix:1 + 2 * n_mix]
    p_ref, gffn_ref, win_ref, wout_ref, wg_ref, wp_ref = refs[1 + 2 * n_mix:7 + 2 * n_mix]
    gfin_ref = refs[7 + 2 * n_mix] if final else None
    o_ref = refs[-1]

    y = None
    for m_ref, w_ref in zip(mix_refs, wmix_refs):
        d = jnp.dot(m_ref[...].astype(BF16), w_ref[...], preferred_element_type=F32)
        y = d if y is None else y + d
    h = x_ref[...] + y
    hn = _rms(h, gffn_ref[...]).astype(BF16)
    d_ff = wout_ref.shape[0]
    acc = None
    for c in range(d_ff // FF_CHUNK):
        lo = c * FF_CHUNK
        gate = jnp.dot(hn, win_ref[:, lo:lo + FF_CHUNK], preferred_element_type=F32)
        up = jnp.dot(hn, win_ref[:, d_ff + lo:d_ff + lo + FF_CHUNK], preferred_element_type=F32)
        act = (gate * jax.nn.sigmoid(gate) * up).astype(BF16)
        d = jnp.dot(act, wout_ref[lo:lo + FF_CHUNK, :], preferred_element_type=F32)
        acc = d if acc is None else acc + d
    h = h + acc
    gate = jax.nn.sigmoid(jnp.dot(h.astype(BF16), wg_ref[...], preferred_element_type=F32))
    h = h + gate * jnp.dot(p_ref[...].astype(BF16), wp_ref[...], preferred_element_type=F32)
    o_ref[...] = _rms(h, gfin_ref[...]) if final else h


def _tail(x, mixes, wmixes, p, gffn, win, wout, wg, wp, gfin, tm, name):
    n, d = x.shape
    assert n % tm == 0 and wout.shape[0] % FF_CHUNK == 0
    tok = lambda w: pl.BlockSpec((tm, w), lambda i: (i, 0))
    final = gfin is not None
    weights = list(wmixes) + [gffn, win, wout, wg, wp] + ([gfin] if final else [])
    args = [x] + list(mixes) + list(wmixes) + [p, gffn, win, wout, wg, wp] + ([gfin] if final else [])
    in_specs = ([tok(d)] + [tok(m.shape[1]) for m in mixes] + [_resident(w.shape) for w in wmixes]
                + [tok(p.shape[1])] + [_resident(w.shape) for w in weights[len(wmixes):]])
    return pl.pallas_call(
        functools.partial(_tail_body, n_mix=len(mixes), final=final),
        grid=(n // tm,),
        in_specs=in_specs,
        out_specs=tok(d),
        out_shape=jax.ShapeDtypeStruct((n, d), F32),
        compiler_params=_params(1),
        name=name,
    )(*args)


def _diag_tiles(w):
    nb, bs, _ = w.shape
    per = LANES // bs
    w4 = w.reshape(nb // per, per, bs, bs)
    eye = jnp.eye(per, dtype=w.dtype)
    return jnp.einsum("kaij,ab->kaibj", w4, eye).reshape(nb // per, LANES, LANES)


def _pad_rows_front(a, rows):
    return jnp.pad(a, ((0, 0), (rows - a.shape[1], 0), (0, 0)))


def _trunk(x, p, cache, w, *, tm, rg_rows, attn_rows, c_rows):
    n_seq, t_len, d = x.shape
    n = n_seq * t_len
    tm = min(tm, n)
    h = x.reshape(n, d)
    depth = p.shape[0]
    outs = dict(a_conv=[], a_h=[], b_k=[], b_v=[], c_conv=[], c_C=[], c_n=[], c_m=[])
    for li in range(depth):
        j = li // 2
        last = li == depth - 1
        pl_i = p[li].reshape(n, -1)
        g_mix = w["norm_mix_g"][li][None, :]
        tail_w = (w["norm_ffn_g"][li][None, :], w["ffn_w_in"][li].astype(BF16), w["ffn_w_out"][li].astype(BF16),
                  w["ple_w_gate"][li].astype(BF16), w["ple_w_proj"][li].astype(BF16))
        gfin = w["final_g"][None, :] if last else None
        if li % 2 == 0:
            a_width = w["ab_lambda"].shape[1]
            kv_w = B_KV_HEADS * B_HEAD_DIM
            q_w = B_HEADS * B_HEAD_DIM
            g, u, q, k, v = _rms_inproj(h, g_mix, w["ab_w_in"][j].astype(BF16),
                                        (a_width, a_width, q_w, kv_w, kv_w), tm, f"l{li}_inproj_{t_len}")
            if cache is None:
                conv0 = jnp.zeros((n_seq, SUBLANES, a_width), F32)
                h0 = jnp.zeros((n_seq, 1, a_width), F32)
            else:
                conv0 = _pad_rows_front(cache["a_conv"][j], SUBLANES)
                h0 = cache["a_h"][j][:, None, :]
            wri = jnp.concatenate([_diag_tiles(w["ab_w_r"][j]), _diag_tiles(w["ab_w_i"][j])], axis=2).astype(BF16)
            ya, h_last = _rglru(u, g, conv0, h0, w["ab_conv_w"][j], w["ab_conv_b"][j][None, :], wri,
                                w["ab_b_r"][j][None, :], w["ab_b_i"][j][None, :], w["ab_lambda"][j][None, :],
                                n_seq, rg_rows, f"l{li}_rglru_{t_len}")
            sinks = w["ab_sinks"][j]
            if cache is None:
                o = _attn_prompt(q, k, v, sinks, n_seq, attn_rows, f"l{li}_attn_{t_len}")
                keep = min(WINDOW, t_len)
            else:
                ck = cache["b_k"][j].reshape(n_seq, -1, kv_w)
                cv = cache["b_v"][j].reshape(n_seq, -1, kv_w)
                o = _attn_sample(q, k, v, ck, cv, sinks, n_seq, f"l{li}_attn_{t_len}")
                keep = t_len
            u3 = u.reshape(n_seq, t_len, a_width)
            outs["a_conv"].append(u3[:, t_len - (CONV_W - 1):])
            outs["a_h"].append(h_last[:, 0])
            outs["b_k"].append(k.reshape(n_seq, t_len, B_KV_HEADS, B_HEAD_DIM)[:, t_len - keep:])
            outs["b_v"].append(v.reshape(n_seq, t_len, B_KV_HEADS, B_HEAD_DIM)[:, t_len - keep:])
            w_out = w["ab_w_out"][j].astype(BF16)
            mixes, wmixes = (ya, o), (w_out[:a_width], w_out[a_width:])
        else:
            c_width = w["c_conv_b"].shape[1]
            dh = c_width // C_HEADS
            w_in = w["c_w_in"][j]
            n_gate = w_in.shape[1] - 3 * c_width
            w_in = jnp.pad(w_in, ((0, 0), (0, LANES - n_gate))).astype(BF16)
            u, v, o, gates = _rms_inproj(h, g_mix, w_in, (c_width, c_width, c_width, LANES), tm,
                                         f"l{li}_inproj_{t_len}")
            nt = t_len // c_rows
            gates_t = gates[:, :SUBLANES].reshape(n_seq * nt, c_rows, SUBLANES).transpose(0, 2, 1)
            if cache is None:
                conv0 = jnp.zeros((n_seq, SUBLANES, c_width), F32)
                c0 = jnp.zeros((n_seq, C_HEADS, dh, dh), F32)
                n0 = jnp.zeros((n_seq, C_HEADS, dh), F32)
                m0 = jnp.zeros((n_seq, C_HEADS, LANES), F32)
            else:
                conv0 = _pad_rows_front(cache["c_conv"][j], SUBLANES)
                c0, n0 = cache["c_C"][j], cache["c_n"][j]
                m0 = jnp.broadcast_to(cache["c_m"][j][:, :, None], (n_seq, C_HEADS, LANES))
            wqk = jnp.concatenate([_diag_tiles(w["c_w_q"][j]), _diag_tiles(w["c_w_k"][j]) * (dh ** -0.5)],
                                  axis=2).astype(BF16)
            bg = jnp.pad(w["c_b_gate"][j], (0, LANES - n_gate))[None, :]
            bgt = w["c_b_gate"][j][:, None]
            hn, c1, n1, m1 = _mlstm(u, v, o, gates, gates_t, conv0, c0, n0, m0,
                                    w["c_conv_w"][j], w["c_conv_b"][j][None, :], wqk, bg, bgt,
                                    w["c_norm_g"][j][None, :], n_seq, c_rows, f"l{li}_mlstm_{t_len}")
            outs["c_conv"].append(u.reshape(n_seq, t_len, c_width)[:, t_len - (CONV_W - 1):])
            outs["c_C"].append(c1)
            outs["c_n"].append(n1)
            outs["c_m"].append(m1[:, :, 0])
            mixes, wmixes = (hn,), (w["c_w_out"][j].astype(BF16),)
        h = _tail(h, mixes, wmixes, pl_i, *tail_w, gfin, tm, f"l{li}_tail_{t_len}")
    y = h.reshape(n_seq, t_len, d)
    return (y,) + tuple(jnp.stack(outs[key]) for key in
                        ("a_conv", "a_h", "b_k", "b_v", "c_conv", "c_C", "c_n", "c_m"))


def kernel(x_prompt, x_sample, cache_a_conv, state_a_h, cache_b_k, cache_b_v, cache_c_conv, state_c_C, state_c_n, state_c_m, p_prompt, p_sample, norm_mix_g, norm_ffn_g, final_g, ab_w_in, ab_conv_w, ab_conv_b, ab_w_r, ab_b_r, ab_w_i, ab_b_i, ab_lambda, ab_sinks, ab_w_out, c_w_in, c_b_gate, c_conv_w, c_conv_b, c_w_q, c_w_k, c_norm_g, c_w_out, ffn_w_in, ffn_w_out, ple_w_proj, ple_w_gate):
    w = dict(norm_mix_g=norm_mix_g, norm_ffn_g=norm_ffn_g, final_g=final_g, ab_w_in=ab_w_in, ab_conv_w=ab_conv_w,
             ab_conv_b=ab_conv_b, ab_w_r=ab_w_r, ab_b_r=ab_b_r, ab_w_i=ab_w_i, ab_b_i=ab_b_i, ab_lambda=ab_lambda,
             ab_sinks=ab_sinks, ab_w_out=ab_w_out, c_w_in=c_w_in, c_b_gate=c_b_gate, c_conv_w=c_conv_w,
             c_conv_b=c_conv_b, c_w_q=c_w_q, c_w_k=c_w_k, c_norm_g=c_norm_g, c_w_out=c_w_out, ffn_w_in=ffn_w_in,
             ffn_w_out=ffn_w_out, ple_w_proj=ple_w_proj, ple_w_gate=ple_w_gate)
    t_p = x_prompt.shape[1]
    t_s = x_sample.shape[1]
    prompt = _trunk(x_prompt, p_prompt, None, w, tm=256, rg_rows=min(512, t_p), attn_rows=min(512, t_p),
                    c_rows=min(256, t_p))
    cache = dict(a_conv=cache_a_conv, a_h=state_a_h, b_k=cache_b_k, b_v=cache_b_v, c_conv=cache_c_conv,
                 c_C=state_c_C, c_n=state_c_n, c_m=state_c_m)
    sample = _trunk(x_sample, p_sample, cache, w, tm=256, rg_rows=t_s, attn_rows=t_s, c_rows=t_s)
    return (prompt[0], sample[0]) + prompt[1:] + sample[1:]
```

```python
import functools

import jax
import jax.numpy as jnp
from jax import lax
from jax.experimental import pallas as pl
from jax.experimental.pallas import tpu as pltpu

F32 = jnp.float32
BF16 = jnp.bfloat16

CHUNK = 64
NORM_EPS = 1e-6
NEG_INF = -1e30
CONV_W = 4
RG_C = 8.0
A_BLOCKS = 8
B_HEADS = 8
B_KV_HEADS = 2
B_HEAD_DIM = 64
B_GROUPS = B_HEADS // B_KV_HEADS
WINDOW = 128
C_HEADS = 4
C_QK_BLK = 4

LANES = 128
SUBLANES = 8
VMEM_LIMIT_BYTES = 56 * 1024 * 1024

_NT = (((1,), (1,)), ((), ()))
_TN = (((0,), (0,)), ((), ()))


def _params(n_grid_axes):
    return pltpu.CompilerParams(
        dimension_semantics=("arbitrary",) * n_grid_axes,
        vmem_limit_bytes=VMEM_LIMIT_BYTES)


def _resident(shape):
    nd = len(shape)
    return pl.BlockSpec(shape, lambda *_: (0,) * nd, pipeline_mode=pl.Buffered(1))


def _rms(x, g):
    return x * lax.rsqrt(jnp.mean(x * x, axis=-1, keepdims=True) + NORM_EPS) * g


def _log_sigmoid(x):
    return jnp.minimum(x, 0.0) - jnp.log1p(jnp.exp(-jnp.abs(x)))


def _softplus(x):
    return jnp.maximum(x, 0.0) + jnp.log1p(jnp.exp(-jnp.abs(x)))


def _split3(x):
    hi = x.astype(BF16)
    r1 = x - hi.astype(F32)
    mid = r1.astype(BF16)
    lo = (r1 - mid.astype(F32)).astype(BF16)
    return hi, mid, lo


def _rms_inproj_body(x_ref, g_ref, w_ref, *out_refs):
    xn = _rms(x_ref[...], g_ref[...]).astype(BF16)
    off = 0
    for o_ref in out_refs:
        wd = o_ref.shape[-1]
        o_ref[...] = jnp.dot(xn, w_ref[:, off:off + wd], preferred_element_type=F32)
        off += wd


def _rms_inproj(x, g, w, widths, tm, name):
    n, d = x.shape
    assert n % tm == 0 and sum(widths) == w.shape[1]
    return pl.pallas_call(
        _rms_inproj_body,
        grid=(n // tm,),
        in_specs=[pl.BlockSpec((tm, d), lambda i: (i, 0)), _resident((1, d)), _resident(w.shape)],
        out_specs=[pl.BlockSpec((tm, wd), lambda i: (i, 0)) for wd in widths],
        out_shape=[jax.ShapeDtypeStruct((n, wd), F32) for wd in widths],
        compiler_params=_params(1),
        name=name,
    )(x, g, w)


def _conv_tile(u, ext_ref, cw_ref, cb_ref, rows):
    ext_ref[SUBLANES:SUBLANES + rows, :] = u
    cw = cw_ref[...]
    out = cb_ref[...] + cw[0:1] * ext_ref[5:5 + rows, :]
    out = out + cw[1:2] * ext_ref[6:6 + rows, :]
    out = out + cw[2:3] * ext_ref[7:7 + rows, :]
    out = out + cw[3:4] * u
    ext_ref[0:SUBLANES, :] = ext_ref[rows:rows + SUBLANES, :]
    return out


def _rglru_body(u_ref, g_ref, conv0_ref, h0_ref, cw_ref, cb_ref, wri_ref, br_ref, bi_ref, lam_ref,
                ya_ref, hlast_ref, ext_ref, a_s, b_s, h_s, carry_ref, *, rows):
    t = pl.program_id(1)

    @pl.when(t == 0)
    def _init():
        ext_ref[0:SUBLANES, :] = conv0_ref[0]
        carry_ref[...] = h0_ref[0]

    uc = _conv_tile(u_ref[...], ext_ref, cw_ref, cb_ref, rows)
    width = uc.shape[1]
    ucb = uc.astype(BF16)
    r_parts, i_parts = [], []
    for kt in range(width // LANES):
        pre = jnp.dot(ucb[:, kt * LANES:(kt + 1) * LANES], wri_ref[kt], preferred_element_type=F32)
        r_parts.append(pre[:, :LANES])
        i_parts.append(pre[:, LANES:])
    r = jax.nn.sigmoid(jnp.concatenate(r_parts, axis=1) + br_ref[...])
    ig = jax.nn.sigmoid(jnp.concatenate(i_parts, axis=1) + bi_ref[...])
    log_a = -RG_C * r * _softplus(-lam_ref[...])
    a = jnp.exp(log_a)
    one_minus_a2 = -jnp.tanh(log_a) * (a * a + 1.0)
    bx = jnp.sqrt(one_minus_a2) * (ig * uc)

    nb = rows // SUBLANES
    a3 = a.reshape(nb, SUBLANES, width)
    b3 = bx.reshape(nb, SUBLANES, width)
    row = lax.broadcasted_iota(jnp.int32, a3.shape, 1)
    for s in (1, 2, 4):
        keep = row >= s
        a_sh = pltpu.roll(a3, s, axis=1)
        b_sh = pltpu.roll(b3, s, axis=1)
        b3 = jnp.where(keep, a3 * b_sh + b3, b3)
        a3 = jnp.where(keep, a3 * a_sh, a3)
    a_s[...] = a3
    b_s[...] = b3

    def group(i, carry):
        hb = a_s[i] * carry + b_s[i]
        h_s[i] = hb
        return hb[SUBLANES - 1:SUBLANES, :]

    carry = lax.fori_loop(0, nb, group, carry_ref[...], unroll=min(nb, 8))
    carry_ref[...] = carry
    hlast_ref[0] = carry
    h = h_s[...].reshape(rows, width)
    ya_ref[...] = h * jax.nn.gelu(g_ref[...])


def _rglru(u, g, conv0, h0, cw, cb, wri, br, bi, lam, n_seq, rows, name):
    n, width = u.shape
    nt = n // (n_seq * rows)
    assert nt * n_seq * rows == n and rows % SUBLANES == 0
    tok = pl.BlockSpec((rows, width), lambda s, t: (s * nt + t, 0))
    return pl.pallas_call(
        functools.partial(_rglru_body, rows=rows),
        grid=(n_seq, nt),
        in_specs=[tok, tok,
                  pl.BlockSpec((1, SUBLANES, width), lambda s, t: (s, 0, 0)),
                  pl.BlockSpec((1, 1, width), lambda s, t: (s, 0, 0)),
                  _resident(cw.shape), _resident(cb.shape), _resident(wri.shape),
                  _resident(br.shape), _resident(bi.shape), _resident(lam.shape)],
        out_specs=[tok, pl.BlockSpec((1, 1, width), lambda s, t: (s, 0, 0))],
        out_shape=[jax.ShapeDtypeStruct((n, width), F32),
                   jax.ShapeDtypeStruct((n_seq, 1, width), F32)],
        scratch_shapes=[pltpu.VMEM((rows + SUBLANES, width), F32),
                        pltpu.VMEM((rows // SUBLANES, SUBLANES, width), F32),
                        pltpu.VMEM((rows // SUBLANES, SUBLANES, width), F32),
                        pltpu.VMEM((rows // SUBLANES, SUBLANES, width), F32),
                        pltpu.VMEM((1, width), F32)],
        compiler_params=_params(2),
        name=name,
    )(u, g, conv0, h0, cw, cb, wri, br, bi, lam)


def _attn_core(q, kf, vf, sinks_ref, key_ok):
    n_q, n_k = q.shape[0], kf.shape[0]
    ri = lax.broadcasted_iota(jnp.int32, (n_q, n_k), 0)
    ci = lax.broadcasted_iota(jnp.int32, (n_q, n_k), 1)
    dist = jnp.abs(ri + WINDOW - ci).astype(F32)
    qb, kb, vb = q.astype(BF16), kf.astype(BF16), vf.astype(BF16)
    outs = []
    for h in range(B_HEADS):
        kv = h // B_GROUPS
        hs = slice(h * B_HEAD_DIM, (h + 1) * B_HEAD_DIM)
        kvs = slice(kv * B_HEAD_DIM, (kv + 1) * B_HEAD_DIM)
        bias = (-(2.0 ** (-8.0 * (h + 1) / B_HEADS))) * dist
        if key_ok is not None:
            bias = jnp.where(key_ok, bias, NEG_INF)
        s = lax.dot_general(qb[:, hs], kb[:, kvs], _NT, preferred_element_type=F32)
        s = s * (B_HEAD_DIM ** -0.5) + bias
        sink = sinks_ref[h]
        m = jnp.maximum(jnp.max(s, axis=-1, keepdims=True), sink)
        p = jnp.exp(s - m)
        den = jnp.sum(p, axis=-1, keepdims=True) + jnp.exp(sink - m)
        o = jnp.dot(p.astype(BF16), vb[:, kvs], preferred_element_type=F32)
        outs.append(o / den)
    return jnp.concatenate(outs, axis=1)


def _attn_prompt_body(sinks_ref, q_ref, k_ref, kh_ref, v_ref, vh_ref, o_ref, *, rows):
    t = pl.program_id(1)
    kf = jnp.concatenate([kh_ref[...], k_ref[...]], axis=0)
    vf = jnp.concatenate([vh_ref[...], v_ref[...]], axis=0)
    sc_rows = 2 * CHUNK
    n_k = sc_rows + WINDOW
    ri = lax.broadcasted_iota(jnp.int32, (sc_rows, n_k), 0)
    ci = lax.broadcasted_iota(jnp.int32, (sc_rows, n_k), 1)
    qc = ri // CHUNK
    jc = ci // CHUNK
    in_band = (jc >= qc) & (jc <= qc + WINDOW // CHUNK)
    for sc in range(rows // sc_rows):
        c0 = t * (rows // CHUNK) + sc * (sc_rows // CHUNK)
        key_ok = in_band & (jc >= WINDOW // CHUNK - c0)
        r0 = sc * sc_rows
        o_ref[r0:r0 + sc_rows, :] = _attn_core(
            q_ref[r0:r0 + sc_rows, :], kf[r0:r0 + n_k], vf[r0:r0 + n_k], sinks_ref, key_ok)


def _attn_prompt(q, k, v, sinks, n_seq, rows, name):
    n, qw = q.shape
    kw = k.shape[1]
    nt = n // (n_seq * rows)
    assert rows % (2 * CHUNK) == 0 and nt * n_seq * rows == n
    per = rows // WINDOW
    main = lambda w: pl.BlockSpec((rows, w), lambda s, t: (s * nt + t, 0))
    halo = pl.BlockSpec((WINDOW, kw), lambda s, t: (jnp.maximum((s * nt + t) * per - 1, 0), 0))
    return pl.pallas_call(
        functools.partial(_attn_prompt_body, rows=rows),
        grid=(n_seq, nt),
        in_specs=[pl.BlockSpec(memory_space=pltpu.SMEM), main(qw), main(kw), halo, main(kw), halo],
        out_specs=main(qw),
        out_shape=jax.ShapeDtypeStruct((n, qw), F32),
        compiler_params=_params(2),
        name=name,
    )(sinks, q, k, k, v, v)


def _attn_sample_body(sinks_ref, q_ref, k_ref, ck_ref, v_ref, cv_ref, o_ref):
    kf = jnp.concatenate([ck_ref[0], k_ref[...]], axis=0)
    vf = jnp.concatenate([cv_ref[0], v_ref[...]], axis=0)
    o_ref[...] = _attn_core(q_ref[...], kf, vf, sinks_ref, None)


def _attn_sample(q, k, v, cache_k, cache_v, sinks, n_seq, name):
    n, qw = q.shape
    kw = k.shape[1]
    rows = n // n_seq
    assert cache_k.shape == (n_seq, WINDOW, kw)
    tok = lambda w: pl.BlockSpec((rows, w), lambda s: (s, 0))
    cache = pl.BlockSpec((1, WINDOW, kw), lambda s: (s, 0, 0))
    return pl.pallas_call(
        _attn_sample_body,
        grid=(n_seq,),
        in_specs=[pl.BlockSpec(memory_space=pltpu.SMEM), tok(qw), tok(kw), cache, tok(kw), cache],
        out_specs=tok(qw),
        out_shape=jax.ShapeDtypeStruct((n, qw), F32),
        compiler_params=_params(1),
        name=name,
    )(sinks, q, k, cache_k, v, cache_v)


def _mlstm_body(u_ref, v_ref, o_ref, g_ref, gt_ref, conv0_ref, c0_ref, n0_ref, m0_ref,
                cw_ref, cb_ref, wqk_ref, bg_ref, bgt_ref, ng_ref,
                hn_ref, c_out_ref, n_out_ref, m_out_ref,
                ext_ref, c_s, n_s, m_s, *, rows):
    t = pl.program_id(1)

    @pl.when(t == 0)
    def _init():
        ext_ref[0:SUBLANES, :] = conv0_ref[0]
        c_s[...] = c0_ref[0]
        n_s[...] = n0_ref[0]
        m_s[...] = m0_ref[0]

    uc = _conv_tile(u_ref[...], ext_ref, cw_ref, cb_ref, rows)
    uc = uc * jax.nn.sigmoid(uc)
    width = uc.shape[1]
    dh = width // C_HEADS
    ucb = uc.astype(BF16)
    q_parts, k_parts = [], []
    for kt in range(width // LANES):
        qk = jnp.dot(ucb[:, kt * LANES:(kt + 1) * LANES], wqk_ref[kt], preferred_element_type=F32)
        q_parts.append(qk[:, :LANES])
        k_parts.append(qk[:, LANES:])
    q = jnp.concatenate(q_parts, axis=1)
    k = jnp.concatenate(k_parts, axis=1)

    gcol = g_ref[...] + bg_ref[...]
    grow = gt_ref[0] + bgt_ref[...]
    ri = lax.broadcasted_iota(jnp.int32, (rows, rows), 0)
    ci = lax.broadcasted_iota(jnp.int32, (rows, rows), 1)
    causal = ci <= ri
    tril = causal.astype(BF16)
    triu = (ri <= ci).astype(BF16)
    fc_col = sum(jnp.dot(tril, part, preferred_element_type=F32) for part in _split3(_log_sigmoid(gcol)))
    fc_row = sum(jnp.dot(part, triu, preferred_element_type=F32) for part in _split3(_log_sigmoid(grow)))

    for h in range(C_HEADS):
        hs = slice(h * dh, (h + 1) * dh)
        f_col = fc_col[:, C_HEADS + h:C_HEADS + h + 1]
        i_col = gcol[:, h:h + 1]
        f_row = fc_row[C_HEADS + h:C_HEADS + h + 1, :]
        i_row = grow[h:h + 1, :]
        m_prev = m_s[h:h + 1, 0:1]
        dlog = jnp.where(causal, f_col - f_row + i_row, -jnp.inf)
        inter = f_col + m_prev
        m_t = jnp.maximum(inter, jnp.max(dlog, axis=-1, keepdims=True))
        dw = jnp.exp(dlog - m_t)
        qh, kh = q[:, hs], k[:, hs]
        qb, vb = qh.astype(BF16), v_ref[:, hs].astype(BF16)
        s = lax.dot_general(qb, kh.astype(BF16), _NT, preferred_element_type=F32) * dw
        w_prev = jnp.exp(inter - m_t)
        c_prev = c_s[h]
        n_prev = n_s[h:h + 1, :]
        num = (jnp.dot(s.astype(BF16), vb, preferred_element_type=F32)
               + w_prev * jnp.dot(qb, c_prev.astype(BF16), preferred_element_type=F32))
        den = (jnp.sum(s, axis=-1, keepdims=True)
               + w_prev * jnp.sum(qh * n_prev, axis=-1, keepdims=True))
        hh = num / jnp.maximum(jnp.abs(den), jnp.exp(-m_t))

        f_last = f_col[rows - 1:rows, :]
        wlog = f_last - f_col + i_col
        m_new = jnp.maximum(f_last + m_prev, jnp.max(wlog, axis=0, keepdims=True))
        kw = jnp.exp(wlog - m_new) * kh
        decay = jnp.exp(f_last + m_prev - m_new)
        c_s[h] = decay * c_prev + lax.dot_general(kw.astype(BF16), vb, _TN, preferred_element_type=F32)
        n_s[h:h + 1, :] = decay * n_prev + jnp.sum(kw, axis=0, keepdims=True)
        m_s[h:h + 1, :] = jnp.broadcast_to(m_new, (1, LANES))

        hg = jax.nn.sigmoid(o_ref[:, hs]) * hh
        mu = jnp.mean(hg, axis=-1, keepdims=True)
        var = jnp.mean(jnp.square(hg - mu), axis=-1, keepdims=True)
        hn_ref[:, hs] = (hg - mu) * lax.rsqrt(var + NORM_EPS) * ng_ref[:, hs]

    @pl.when(t == pl.num_programs(1) - 1)
    def _finish():
        c_out_ref[0] = c_s[...]
        n_out_ref[0] = n_s[...]
        m_out_ref[0] = m_s[...]


def _mlstm(u, v, o, gates, gates_t, conv0, c0, n0, m0, cw, cb, wqk, bg, bgt, ng, n_seq, rows, name):
    n, width = u.shape
    nt = n // (n_seq * rows)
    dh = width // C_HEADS
    assert nt * n_seq * rows == n and gates_t.shape == (n_seq * nt, SUBLANES, rows)
    tok = lambda w: pl.BlockSpec((rows, w), lambda s, t: (s * nt + t, 0))
    per_seq = lambda shape: pl.BlockSpec((1,) + shape, lambda s, t: (s,) + (0,) * len(shape))
    return pl.pallas_call(
        functools.partial(_mlstm_body, rows=rows),
        grid=(n_seq, nt),
        in_specs=[tok(width), tok(width), tok(width), tok(LANES),
                  pl.BlockSpec((1, SUBLANES, rows), lambda s, t: (s * nt + t, 0, 0)),
                  per_seq((SUBLANES, width)), per_seq((C_HEADS, dh, dh)),
                  per_seq((C_HEADS, dh)), per_seq((C_HEADS, LANES)),
                  _resident(cw.shape), _resident(cb.shape), _resident(wqk.shape),
                  _resident(bg.shape), _resident(bgt.shape), _resident(ng.shape)],
        out_specs=[tok(width), per_seq((C_HEADS, dh, dh)), per_seq((C_HEADS, dh)), per_seq((C_HEADS, LANES))],
        out_shape=[jax.ShapeDtypeStruct((n, width), F32),
                   jax.ShapeDtypeStruct((n_seq, C_HEADS, dh, dh), F32),
                   jax.ShapeDtypeStruct((n_seq, C_HEADS, dh), F32),
                   jax.ShapeDtypeStruct((n_seq, C_HEADS, LANES), F32)],
        scratch_shapes=[pltpu.VMEM((rows + SUBLANES, width), F32),
                        pltpu.VMEM((C_HEADS, dh, dh), F32),
                        pltpu.VMEM((C_HEADS, dh), F32),
                        pltpu.VMEM((C_HEADS, LANES), F32)],
        compiler_params=_params(2),
        name=name,
    )(u, v, o, gates, gates_t, conv0, c0, n0, m0, cw, cb, wqk, bg, bgt, ng)


FF_CHUNK = 256


def _tail_body(*refs, n_mix, final):
    x_ref = refs[0]
    mix_refs = refs[1:1 + n_mix]
    wmix_refs = refs[1 + n_mix:1 + 2 * n_mix]
    p_ref, gffn_ref, win_ref, wout_ref, wg_ref, wp_ref = refs[1 + 2 * n_mix:7 + 2 * n_mix]
    gfin_ref = refs[7 + 2 * n_mix] if final else None
    o_ref = refs[-1]

    y = None
    for m_ref, w_ref in zip(mix_refs, wmix_refs):
        d = jnp.dot(m_ref[...].astype(BF16), w_ref[...], preferred_element_type=F32)
        y = d if y is None else y + d
    h = x_ref[...] + y
    hn = _rms(h, gffn_ref[...]).astype(BF16)
    d_ff = wout_ref.shape[0]
    acc = None
    for c in range(d_ff // FF_CHUNK):
        lo = c * FF_CHUNK
        gate = jnp.dot(hn, win_ref[:, lo:lo + FF_CHUNK], preferred_element_type=F32)
        up = jnp.dot(hn, win_ref[:, d_ff + lo:d_ff + lo + FF_CHUNK], preferred_element_type=F32)
        act = (gate * jax.nn.sigmoid(gate) * up).astype(BF16)
        d = jnp.dot(act, wout_ref[lo:lo + FF_CHUNK, :], preferred_element_type=F32)
        acc = d if acc is None else acc + d
    h = h + acc
    gate = jax.nn.sigmoid(jnp.dot(h.astype(BF16), wg_ref[...], preferred_element_type=F32))
    h = h + gate * jnp.dot(p_ref[...].astype(BF16), wp_ref[...], preferred_element_type=F32)
    o_ref[...] = _rms(h, gfin_ref[...]) if final else h


def _tail(x, mixes, wmixes, p, gffn, win, wout, wg, wp, gfin, tm, name):
    n, d = x.shape
    assert n % tm == 0 and wout.shape[0] % FF_CHUNK == 0
    tok = lambda w: pl.BlockSpec((tm, w), lambda i: (i, 0))
    final = gfin is not None
    weights = list(wmixes) + [gffn, win, wout, wg, wp] + ([gfin] if final else [])
    args = [x] + list(mixes) + list(wmixes) + [p, gffn, win, wout, wg, wp] + ([gfin] if final else [])
    in_specs = ([tok(d)] + [tok(m.shape[1]) for m in mixes] + [_resident(w.shape) for w in wmixes]
                + [tok(p.shape[1])] + [_resident(w.shape) for w in weights[len(wmixes):]])
    return pl.pallas_call(
        functools.partial(_tail_body, n_mix=len(mixes), final=final),
        grid=(n // tm,),
        in_specs=in_specs,
        out_specs=tok(d),
        out_shape=jax.ShapeDtypeStruct((n, d), F32),
        compiler_params=_params(1),
        name=name,
    )(*args)


def _diag_tiles(w):
    nb, bs, _ = w.shape
    per = LANES // bs
    w4 = w.reshape(nb // per, per, bs, bs)
    eye = jnp.eye(per, dtype=w.dtype)
    return jnp.einsum("kaij,ab->kaibj", w4, eye).reshape(nb // per, LANES, LANES)


def _pad_rows_front(a, rows):
    return jnp.pad(a, ((0, 0), (rows - a.shape[1], 0), (0, 0)))


def _trunk(x, p, cache, w, *, tm, rg_rows, attn_rows, c_rows):
    n_seq, t_len, d = x.shape
    n = n_seq * t_len
    tm = min(tm, n)
    h = x.reshape(n, d)
    depth = p.shape[0]
    outs = dict(a_conv=[], a_h=[], b_k=[], b_v=[], c_conv=[], c_C=[], c_n=[], c_m=[])
    for li in range(depth):
        j = li // 2
        last = li == depth - 1
        pl_i = p[li].reshape(n, -1)
        g_mix = w["norm_mix_g"][li][None, :]
        tail_w = (w["norm_ffn_g"][li][None, :], w["ffn_w_in"][li].astype(BF16), w["ffn_w_out"][li].astype(BF16),
                  w["ple_w_gate"][li].astype(BF16), w["ple_w_proj"][li].astype(BF16))
        gfin = w["final_g"][None, :] if last else None
        if li % 2 == 0:
            a_width = w["ab_lambda"].shape[1]
            kv_w = B_KV_HEADS * B_HEAD_DIM
            q_w = B_HEADS * B_HEAD_DIM
            g, u, q, k, v = _rms_inproj(h, g_mix, w["ab_w_in"][j].astype(BF16),
                                        (a_width, a_width, q_w, kv_w, kv_w), tm, f"l{li}_inproj_{t_len}")
            if cache is None:
                conv0 = jnp.zeros((n_seq, SUBLANES, a_width), F32)
                h0 = jnp.zeros((n_seq, 1, a_width), F32)
            else:
                conv0 = _pad_rows_front(cache["a_conv"][j], SUBLANES)
                h0 = cache["a_h"][j][:, None, :]
            wri = jnp.concatenate([_diag_tiles(w["ab_w_r"][j]), _diag_tiles(w["ab_w_i"][j])], axis=2).astype(BF16)
            ya, h_last = _rglru(u, g, conv0, h0, w["ab_conv_w"][j], w["ab_conv_b"][j][None, :], wri,
                                w["ab_b_r"][j][None, :], w["ab_b_i"][j][None, :], w["ab_lambda"][j][None, :],
                                n_seq, rg_rows, f"l{li}_rglru_{t_len}")
            sinks = w["ab_sinks"][j]
            if cache is None:
                o = _attn_prompt(q, k, v, sinks, n_seq, attn_rows, f"l{li}_attn_{t_len}")
                keep = min(WINDOW, t_len)
            else:
                ck = cache["b_k"][j].reshape(n_seq, -1, kv_w)
                cv = cache["b_v"][j].reshape(n_seq, -1, kv_w)
                o = _attn_sample(q, k, v, ck, cv, sinks, n_seq, f"l{li}_attn_{t_len}")
                keep = t_len
            u3 = u.reshape(n_seq, t_len, a_width)
            outs["a_conv"].append(u3[:, t_len - (CONV_W - 1):])
            outs["a_h"].append(h_last[:, 0])
            for key, a in (("b_k", k), ("b_v", v)):
                tail_rows = a.reshape(n_seq, t_len, kv_w)[:, t_len - keep:]
                outs[key].append(tail_rows.reshape(n_seq, keep, B_KV_HEADS, B_HEAD_DIM))
            w_out = w["ab_w_out"][j].astype(BF16)
            mixes, wmixes = (ya, o), (w_out[:a_width], w_out[a_width:])
        else:
            c_width = w["c_conv_b"].shape[1]
            dh = c_width // C_HEADS
            w_in = w["c_w_in"][j]
            n_gate = w_in.shape[1] - 3 * c_width
            w_in = jnp.pad(w_in, ((0, 0), (0, LANES - n_gate))).astype(BF16)
            u, v, o, gates = _rms_inproj(h, g_mix, w_in, (c_width, c_width, c_width, LANES), tm,
                                         f"l{li}_inproj_{t_len}")
            nt = t_len // c_rows
            gates_t = gates[:, :SUBLANES].reshape(n_seq * nt, c_rows, SUBLANES).transpose(0, 2, 1)
            if cache is None:
                conv0 = jnp.zeros((n_seq, SUBLANES, c_width), F32)
                c0 = jnp.zeros((n_seq, C_HEADS, dh, dh), F32)
                n0 = jnp.zeros((n_seq, C_HEADS, dh), F32)
                m0 = jnp.zeros((n_seq, C_HEADS, LANES), F32)
            else:
                conv0 = _pad_rows_front(cache["c_conv"][j], SUBLANES)
                c0, n0 = cache["c_C"][j], cache["c_n"][j]
                m0 = jnp.broadcast_to(cache["c_m"][j][:, :, None], (n_seq, C_HEADS, LANES))
            wqk = jnp.concatenate([_diag_tiles(w["c_w_q"][j]), _diag_tiles(w["c_w_k"][j]) * (dh ** -0.5)],
                                  axis=2).astype(BF16)
            bg = jnp.pad(w["c_b_gate"][j], (0, LANES - n_gate))[None, :]
            bgt = w["c_b_gate"][j][:, None]
            hn, c1, n1, m1 = _mlstm(u, v, o, gates, gates_t, conv0, c0, n0, m0,
                                    w["c_conv_w"][j], w["c_conv_b"][j][None, :], wqk, bg, bgt,
                                    w["c_norm_g"][j][None, :], n_seq, c_rows, f"l{li}_mlstm_{t_len}")
            outs["c_conv"].append(u.reshape(n_seq, t_len, c_width)[:, t_len - (CONV_W - 1):])
            outs["c_C"].append(c1)
            outs["c_n"].append(n1)
            outs["c_m"].append(m1[:, :, 0])
            mixes, wmixes = (hn,), (w["c_w_out"][j].astype(BF16),)
        h = _tail(h, mixes, wmixes, pl_i, *tail_w, gfin, tm, f"l{li}_tail_{t_len}")
    y = h.reshape(n_seq, t_len, d)
    return (y,) + tuple(jnp.stack(outs[key]) for key in
                        ("a_conv", "a_h", "b_k", "b_v", "c_conv", "c_C", "c_n", "c_m"))


def kernel(x_prompt, x_sample, cache_a_conv, state_a_h, cache_b_k, cache_b_v, cache_c_conv, state_c_C, state_c_n, state_c_m, p_prompt, p_sample, norm_mix_g, norm_ffn_g, final_g, ab_w_in, ab_conv_w, ab_conv_b, ab_w_r, ab_b_r, ab_w_i, ab_b_i, ab_lambda, ab_sinks, ab_w_out, c_w_in, c_b_gate, c_conv_w, c_conv_b, c_w_q, c_w_k, c_norm_g, c_w_out, ffn_w_in, ffn_w_out, ple_w_proj, ple_w_gate):
    w = dict(norm_mix_g=norm_mix_g, norm_ffn_g=norm_ffn_g, final_g=final_g, ab_w_in=ab_w_in, ab_conv_w=ab_conv_w,
             ab_conv_b=ab_conv_b, ab_w_r=ab_w_r, ab_b_r=ab_b_r, ab_w_i=ab_w_i, ab_b_i=ab_b_i, ab_lambda=ab_lambda,
             ab_sinks=ab_sinks, ab_w_out=ab_w_out, c_w_in=c_w_in, c_b_gate=c_b_gate, c_conv_w=c_conv_w,
             c_conv_b=c_conv_b, c_w_q=c_w_q, c_w_k=c_w_k, c_norm_g=c_norm_g, c_w_out=c_w_out, ffn_w_in=ffn_w_in,
             ffn_w_out=ffn_w_out, ple_w_proj=ple_w_proj, ple_w_gate=ple_w_gate)
    t_p = x_prompt.shape[1]
    t_s = x_sample.shape[1]
    prompt = _trunk(x_prompt, p_prompt, None, w, tm=512, rg_rows=min(512, t_p), attn_rows=min(512, t_p),
                    c_rows=min(256, t_p))
    cache = dict(a_conv=cache_a_conv, a_h=state_a_h, b_k=cache_b_k, b_v=cache_b_v, c_conv=cache_c_conv,
                 c_C=state_c_C, c_n=state_c_n, c_m=state_c_m)
    sample = _trunk(x_sample, p_sample, cache, w, tm=256, rg_rows=t_s, attn_rows=t_s, c_rows=t_s)
    return (prompt[0], sample[0]) + prompt[1:] + sample[1:]
```

```python
import functools

import jax
import jax.numpy as jnp
from jax import lax
from jax.experimental import pallas as pl
from jax.experimental.pallas import tpu as pltpu

F32 = jnp.float32
BF16 = jnp.bfloat16

CHUNK = 64
NORM_EPS = 1e-6
NEG_INF = -1e30
CONV_W = 4
RG_C = 8.0
B_HEADS = 8
B_KV_HEADS = 2
B_HEAD_DIM = 64
B_GROUPS = B_HEADS // B_KV_HEADS
WINDOW = 128
C_HEADS = 4

LANES = 128
SUBLANES = 8
VMEM_LIMIT_BYTES = 60 * 1024 * 1024
FF_CHUNK = 256

_NT = (((1,), (1,)), ((), ()))
_TN = (((0,), (0,)), ((), ()))


def _resident(shape):
    nd = len(shape)
    return pl.BlockSpec(shape, lambda *_: (0,) * nd, pipeline_mode=pl.Buffered(1))


def _rms(x, g):
    return x * lax.rsqrt(jnp.mean(x * x, axis=-1, keepdims=True) + NORM_EPS) * g


def _log_sigmoid(x):
    return jnp.minimum(x, 0.0) - jnp.log1p(jnp.exp(-jnp.abs(x)))


def _softplus(x):
    return jnp.maximum(x, 0.0) + jnp.log1p(jnp.exp(-jnp.abs(x)))


def _split3(x):
    hi = x.astype(BF16)
    r1 = x - hi.astype(F32)
    mid = r1.astype(BF16)
    lo = (r1 - mid.astype(F32)).astype(BF16)
    return hi, mid, lo


def _mm(a, b):
    return jnp.dot(a, b, preferred_element_type=F32)


def _conv_tile(u, ext_ref, cw_ref, cb_ref, rows):
    ext_ref[SUBLANES:SUBLANES + rows, :] = u
    cw = cw_ref[...]
    out = cb_ref[...] + cw[0:1] * ext_ref[5:5 + rows, :]
    out = out + cw[1:2] * ext_ref[6:6 + rows, :]
    out = out + cw[2:3] * ext_ref[7:7 + rows, :]
    out = out + cw[3:4] * u
    ext_ref[0:SUBLANES, :] = ext_ref[rows:rows + SUBLANES, :]
    return out


def _rglru_tile(u, g, ext_ref, carry_ref, cw_ref, cb_ref, wri_ref, br_ref, bi_ref, lam_ref, rows):
    uc = _conv_tile(u, ext_ref, cw_ref, cb_ref, rows)
    width = uc.shape[1]
    ucb = uc.astype(BF16)
    r_parts, i_parts = [], []
    for kt in range(width // LANES):
        pre = _mm(ucb[:, kt * LANES:(kt + 1) * LANES], wri_ref[kt])
        r_parts.append(pre[:, :LANES])
        i_parts.append(pre[:, LANES:])
    r = jax.nn.sigmoid(jnp.concatenate(r_parts, axis=1) + br_ref[...])
    ig = jax.nn.sigmoid(jnp.concatenate(i_parts, axis=1) + bi_ref[...])
    log_a = -RG_C * r * _softplus(-lam_ref[...])
    a = jnp.exp(log_a)
    bx = jnp.sqrt(-jnp.tanh(log_a) * (a * a + 1.0)) * (ig * uc)

    nb = rows // SUBLANES
    a3 = a.reshape(nb, SUBLANES, width)
    b3 = bx.reshape(nb, SUBLANES, width)
    row = lax.broadcasted_iota(jnp.int32, a3.shape, 1)
    for s in (1, 2, 4):
        keep = row >= s
        a_sh = pltpu.roll(a3, s, axis=1)
        b_sh = pltpu.roll(b3, s, axis=1)
        b3 = jnp.where(keep, a3 * b_sh + b3, b3)
        a3 = jnp.where(keep, a3 * a_sh, a3)
    carry = carry_ref[...]
    groups = []
    for i in range(nb):
        hb = a3[i] * carry + b3[i]
        groups.append(hb)
        carry = hb[SUBLANES - 1:SUBLANES, :]
    carry_ref[...] = carry
    h = jnp.concatenate(groups, axis=0)
    return h * jax.nn.gelu(g)


def _attn_core(q, kf, vf, sinks_ref, key_ok):
    n_q, n_k = q.shape[0], kf.shape[0]
    ri = lax.broadcasted_iota(jnp.int32, (n_q, n_k), 0)
    ci = lax.broadcasted_iota(jnp.int32, (n_q, n_k), 1)
    dist = jnp.abs(ri + WINDOW - ci).astype(F32)
    qb, kb, vb = q.astype(BF16), kf.astype(BF16), vf.astype(BF16)
    outs = []
    for h in range(B_HEADS):
        kv = h // B_GROUPS
        hs = slice(h * B_HEAD_DIM, (h + 1) * B_HEAD_DIM)
        kvs = slice(kv * B_HEAD_DIM, (kv + 1) * B_HEAD_DIM)
        bias = (-(2.0 ** (-8.0 * (h + 1) / B_HEADS))) * dist
        if key_ok is not None:
            bias = jnp.where(key_ok, bias, NEG_INF)
        s = lax.dot_general(qb[:, hs], kb[:, kvs], _NT, preferred_element_type=F32)
        s = s * (B_HEAD_DIM ** -0.5) + bias
        sink = sinks_ref[h]
        m = jnp.maximum(jnp.max(s, axis=-1, keepdims=True), sink)
        p = jnp.exp(s - m)
        den = jnp.sum(p, axis=-1, keepdims=True) + jnp.exp(sink - m)
        outs.append(_mm(p.astype(BF16), vb[:, kvs]) / den)
    return jnp.concatenate(outs, axis=1)


def _attn_band_tile(q, k, v, kf_ref, vf_ref, sinks_ref, t, rows):
    kf_ref[WINDOW:WINDOW + rows, :] = k
    vf_ref[WINDOW:WINDOW + rows, :] = v
    sc_rows = 2 * CHUNK
    n_k = sc_rows + WINDOW
    ri = lax.broadcasted_iota(jnp.int32, (sc_rows, n_k), 0)
    ci = lax.broadcasted_iota(jnp.int32, (sc_rows, n_k), 1)
    qc = ri // CHUNK
    jc = ci // CHUNK
    in_band = (jc >= qc) & (jc <= qc + WINDOW // CHUNK)
    outs = []
    for sc in range(rows // sc_rows):
        c0 = t * (rows // CHUNK) + sc * (sc_rows // CHUNK)
        key_ok = in_band & (jc >= WINDOW // CHUNK - c0)
        r0 = sc * sc_rows
        outs.append(_attn_core(q[r0:r0 + sc_rows], kf_ref[r0:r0 + n_k, :], vf_ref[r0:r0 + n_k, :],
                               sinks_ref, key_ok))
    kf_ref[0:WINDOW, :] = kf_ref[rows:rows + WINDOW, :]
    vf_ref[0:WINDOW, :] = vf_ref[rows:rows + WINDOW, :]
    return jnp.concatenate(outs, axis=0)


def _mlstm_tile(u, v, o, gcol, grow, ext_ref, c_s, n_s, m_s, cw_ref, cb_ref, wqk_ref, ng_ref, rows):
    uc = _conv_tile(u, ext_ref, cw_ref, cb_ref, rows)
    uc = uc * jax.nn.sigmoid(uc)
    width = uc.shape[1]
    dh = width // C_HEADS
    ucb = uc.astype(BF16)
    q_parts, k_parts = [], []
    for kt in range(width // LANES):
        qk = _mm(ucb[:, kt * LANES:(kt + 1) * LANES], wqk_ref[kt])
        q_parts.append(qk[:, :LANES])
        k_parts.append(qk[:, LANES:])
    q = jnp.concatenate(q_parts, axis=1)
    k = jnp.concatenate(k_parts, axis=1)

    ri = lax.broadcasted_iota(jnp.int32, (rows, rows), 0)
    ci = lax.broadcasted_iota(jnp.int32, (rows, rows), 1)
    causal = ci <= ri
    tril = causal.astype(BF16)
    triu = (ri <= ci).astype(BF16)
    fc_col = sum(_mm(tril, part) for part in _split3(_log_sigmoid(gcol)))
    fc_row = sum(_mm(part, triu) for part in _split3(_log_sigmoid(grow)))

    outs = []
    for h in range(C_HEADS):
        hs = slice(h * dh, (h + 1) * dh)
        f_col = fc_col[:, C_HEADS + h:C_HEADS + h + 1]
        i_col = gcol[:, h:h + 1]
        f_row = fc_row[C_HEADS + h:C_HEADS + h + 1, :]
        i_row = grow[h:h + 1, :]
        m_prev = m_s[h:h + 1, 0:1]
        dlog = jnp.where(causal, f_col - f_row + i_row, -jnp.inf)
        inter = f_col + m_prev
        m_t = jnp.maximum(inter, jnp.max(dlog, axis=-1, keepdims=True))
        dw = jnp.exp(dlog - m_t)
        qh, kh = q[:, hs], k[:, hs]
        qb, vb = qh.astype(BF16), v[:, hs].astype(BF16)
        s = lax.dot_general(qb, kh.astype(BF16), _NT, preferred_element_type=F32) * dw
        w_prev = jnp.exp(inter - m_t)
        c_prev = c_s[h]
        n_prev = n_s[h:h + 1, :]
        num = _mm(s.astype(BF16), vb) + w_prev * _mm(qb, c_prev.astype(BF16))
        den = (jnp.sum(s, axis=-1, keepdims=True)
               + w_prev * jnp.sum(qh * n_prev, axis=-1, keepdims=True))
        hh = num / jnp.maximum(jnp.abs(den), jnp.exp(-m_t))

        f_last = f_col[rows - 1:rows, :]
        wlog = f_last - f_col + i_col
        m_new = jnp.maximum(f_last + m_prev, jnp.max(wlog, axis=0, keepdims=True))
        kw = jnp.exp(wlog - m_new) * kh
        decay = jnp.exp(f_last + m_prev - m_new)
        c_s[h] = lax.dot_general(kw.astype(BF16), vb, _TN, preferred_element_type=F32) + decay * c_prev
        n_s[h:h + 1, :] = decay * n_prev + jnp.sum(kw, axis=0, keepdims=True)
        m_s[h:h + 1, :] = jnp.broadcast_to(m_new, (1, LANES))

        hg = jax.nn.sigmoid(o[:, hs]) * hh
        mu = jnp.mean(hg, axis=-1, keepdims=True)
        var = jnp.mean(jnp.square(hg - mu), axis=-1, keepdims=True)
        outs.append((hg - mu) * lax.rsqrt(var + NORM_EPS) * ng_ref[:, hs])
    return jnp.concatenate(outs, axis=1)


def _tail_tile(x, mixes, wmix_refs, p, gffn_ref, win_ref, wout_ref, wg_ref, wp_ref, gfin_ref):
    y = None
    for m, w_ref in zip(mixes, wmix_refs):
        d = _mm(m.astype(BF16), w_ref[...])
        y = d if y is None else y + d
    h = x + y
    hn = _rms(h, gffn_ref[...]).astype(BF16)
    d_ff = wout_ref.shape[0]
    acc = None
    for c in range(d_ff // FF_CHUNK):
        lo = c * FF_CHUNK
        gate = _mm(hn, win_ref[:, lo:lo + FF_CHUNK])
        up = _mm(hn, win_ref[:, d_ff + lo:d_ff + lo + FF_CHUNK])
        act = (gate * jax.nn.sigmoid(gate) * up).astype(BF16)
        d = _mm(act, wout_ref[lo:lo + FF_CHUNK, :])
        acc = d if acc is None else acc + d
    h = h + acc
    h = h + jax.nn.sigmoid(_mm(h.astype(BF16), wg_ref[...])) * _mm(p.astype(BF16), wp_ref[...])
    return h if gfin_ref is None else _rms(h, gfin_ref[...])


def _stage_indices(n_tiles, nt):
    i = pl.program_id(0)
    tile_m = jnp.clip(i - 1, 0, n_tiles - 1)
    t_m = lax.rem(tile_m, nt)
    seq_end = jnp.logical_and(t_m == nt - 1, jnp.logical_and(i >= 1, i <= n_tiles))
    return i, t_m, seq_end


def _layer0_body(*refs, rows, nt, n_tiles, has_cache, final):
    it = iter(refs)
    sinks_ref, xp_ref, xt_ref, p_ref = next(it), next(it), next(it), next(it)
    if has_cache:
        conv0_ref, h0_ref, ck_ref, cv_ref = next(it), next(it), next(it), next(it)
    (gmix_ref, win_ref, cw_ref, cb_ref, wri_ref, br_ref, bi_ref, lam_ref,
     woa_ref, wob_ref, gffn_ref, fin_ref, fout_ref, wg_ref, wp_ref) = (next(it) for _ in range(15))
    gfin_ref = next(it) if final else None
    y_ref, ulast_ref, klast_ref, vlast_ref, hlast_ref = (next(it) for _ in range(5))
    g_s, u_s, q_s, k_s, v_s, ya_s, o_s, ext_ref, carry_ref = (next(it) for _ in range(9))
    if not has_cache:
        kf_ref, vf_ref = next(it), next(it)
    i, t_m, seq_end = _stage_indices(n_tiles, nt)

    @pl.when(i == 0)
    def _first():
        for ref in (g_s, u_s, q_s, k_s, v_s, ya_s, o_s):
            ref[...] = jnp.zeros_like(ref)

    @pl.when(t_m == 0)
    def _init():
        if has_cache:
            ext_ref[0:SUBLANES, :] = conv0_ref[0]
            carry_ref[...] = h0_ref[0]
        else:
            ext_ref[0:SUBLANES, :] = jnp.zeros((SUBLANES, ext_ref.shape[1]), F32)
            carry_ref[...] = jnp.zeros_like(carry_ref)
            kf_ref[0:WINDOW, :] = jnp.zeros((WINDOW, kf_ref.shape[1]), F32)
            vf_ref[0:WINDOW, :] = jnp.zeros((WINDOW, vf_ref.shape[1]), F32)

    y_ref[...] = _tail_tile(xt_ref[...], (ya_s[...], o_s[...]), (woa_ref, wob_ref), p_ref[...],
                            gffn_ref, fin_ref, fout_ref, wg_ref, wp_ref, gfin_ref)

    q, k, v = q_s[...], k_s[...], v_s[...]
    ya = _rglru_tile(u_s[...], g_s[...], ext_ref, carry_ref, cw_ref, cb_ref, wri_ref, br_ref, bi_ref, lam_ref,
                     rows)
    if has_cache:
        kf = jnp.concatenate([ck_ref[0], k], axis=0)
        vf = jnp.concatenate([cv_ref[0], v], axis=0)
        o = _attn_core(q, kf, vf, sinks_ref, None)
    else:
        o = _attn_band_tile(q, k, v, kf_ref, vf_ref, sinks_ref, t_m, rows)
    ya_s[...] = ya
    o_s[...] = o

    a_w = lam_ref.shape[1]
    q_w = B_HEADS * B_HEAD_DIM
    kv_w = B_KV_HEADS * B_HEAD_DIM
    xn = _rms(xp_ref[...], gmix_ref[...]).astype(BF16)
    off = 0
    for ref, wd in ((g_s, a_w), (u_s, a_w), (q_s, q_w), (k_s, kv_w), (v_s, kv_w)):
        ref[...] = _mm(xn, win_ref[:, off:off + wd])
        off += wd
    keep = klast_ref.shape[1]
    ulast_ref[0] = u_s[rows - SUBLANES:rows, :]
    klast_ref[0] = k_s[rows - keep:rows, :]
    vlast_ref[0] = v_s[rows - keep:rows, :]

    @pl.when(seq_end)
    def _seq_end():
        hlast_ref[0] = carry_ref[...]


def _layer1_body(*refs, rows, nt, n_tiles, has_cache, final):
    it = iter(refs)
    xp_ref, xt_ref, p_ref = next(it), next(it), next(it)
    if has_cache:
        conv0_ref, c0_ref, n0_ref, m0_ref = next(it), next(it), next(it), next(it)
    (gmix_ref, win_ref, wgt_ref, cw_ref, cb_ref, wqk_ref, bg_ref, bgt_ref, ng_ref,
     wo_ref, gffn_ref, fin_ref, fout_ref, wg_ref, wp_ref) = (next(it) for _ in range(15))
    gfin_ref = next(it) if final else None
    y_ref, ulast_ref, c_out_ref, n_out_ref, m_out_ref = (next(it) for _ in range(5))
    u_s, v_s, o_s, gcol_s, grow_s, hn_s, ext_ref, c_s, n_s, m_s = (next(it) for _ in range(10))
    i, t_m, seq_end = _stage_indices(n_tiles, nt)

    @pl.when(i == 0)
    def _first():
        for ref in (u_s, v_s, o_s, gcol_s, grow_s, hn_s):
            ref[...] = jnp.zeros_like(ref)

    @pl.when(t_m == 0)
    def _init():
        if has_cache:
            ext_ref[0:SUBLANES, :] = conv0_ref[0]
            c_s[...] = c0_ref[0]
            n_s[...] = n0_ref[0]
            m_s[...] = m0_ref[0]
        else:
            ext_ref[0:SUBLANES, :] = jnp.zeros((SUBLANES, ext_ref.shape[1]), F32)
            c_s[...] = jnp.zeros_like(c_s)
            n_s[...] = jnp.zeros_like(n_s)
            m_s[...] = jnp.zeros_like(m_s)

    y_ref[...] = _tail_tile(xt_ref[...], (hn_s[...],), (wo_ref,), p_ref[...],
                            gffn_ref, fin_ref, fout_ref, wg_ref, wp_ref, gfin_ref)

    hn_s[...] = _mlstm_tile(u_s[...], v_s[...], o_s[...], gcol_s[...], grow_s[...], ext_ref, c_s, n_s, m_s,
                            cw_ref, cb_ref, wqk_ref, ng_ref, rows)

    width = ng_ref.shape[1]
    xn = _rms(xp_ref[...], gmix_ref[...]).astype(BF16)
    for j, ref in enumerate((u_s, v_s, o_s)):
        ref[...] = _mm(xn, win_ref[:, j * width:(j + 1) * width])
    gcol_s[...] = _mm(xn, win_ref[:, 3 * width:3 * width + LANES]) + bg_ref[...]
    grow_s[...] = lax.dot_general(wgt_ref[...], xn, _NT, preferred_element_type=F32) + bgt_ref[...]
    ulast_ref[0] = u_s[rows - SUBLANES:rows, :]

    @pl.when(seq_end)
    def _seq_end():
        c_out_ref[0] = c_s[...]
        n_out_ref[0] = n_s[...]
        m_out_ref[0] = m_s[...]


def _layer_call(body, first_inputs, x, p, state_inputs, weights, proj_state_shapes, mixer_state_shapes, scratch,
                n_seq, rows, name, final):
    n, d = x.shape
    nt = n // (n_seq * rows)
    n_tiles = n_seq * nt
    assert n_tiles * rows == n
    proj_tile_of = lambda i: jnp.minimum(i, n_tiles - 1)
    mixer_tile_of = lambda i: jnp.clip(i - 1, 0, n_tiles - 1)
    tail_tile_of = lambda i: jnp.maximum(i - 2, 0)
    tile = lambda w, tile_of: pl.BlockSpec((rows, w), lambda i: (tile_of(i), 0))
    per_seq = lambda shape, tile_of: pl.BlockSpec(
        (1,) + tuple(shape), lambda i: (tile_of(i) // nt,) + (0,) * len(shape))
    in_specs = ([pl.BlockSpec(memory_space=pltpu.SMEM) for _ in first_inputs]
                + [tile(d, proj_tile_of), tile(d, tail_tile_of), tile(p.shape[1], tail_tile_of)]
                + [per_seq(s.shape[1:], mixer_tile_of) for s in state_inputs]
                + [_resident(w.shape) for w in weights])
    out_specs = ([tile(d, tail_tile_of)] + [per_seq(s[1:], proj_tile_of) for s in proj_state_shapes]
                 + [per_seq(s[1:], mixer_tile_of) for s in mixer_state_shapes])
    out_shape = ([jax.ShapeDtypeStruct((n, d), F32)]
                 + [jax.ShapeDtypeStruct(s, F32) for s in list(proj_state_shapes) + list(mixer_state_shapes)])
    return pl.pallas_call(
        functools.partial(body, rows=rows, nt=nt, n_tiles=n_tiles, has_cache=bool(state_inputs), final=final),
        grid=(n_tiles + 2,),
        in_specs=in_specs,
        out_specs=out_specs,
        out_shape=out_shape,
        scratch_shapes=scratch,
        compiler_params=pltpu.CompilerParams(dimension_semantics=("arbitrary",),
                                             vmem_limit_bytes=VMEM_LIMIT_BYTES),
        name=name,
    )(*first_inputs, x, x, p, *state_inputs, *weights)


def _diag_tiles(w):
    nb, bs, _ = w.shape
    per = LANES // bs
    w4 = w.reshape(nb // per, per, bs, bs)
    eye = jnp.eye(per, dtype=w.dtype)
    return jnp.einsum("kaij,ab->kaibj", w4, eye).reshape(nb // per, LANES, LANES)


def _pad_rows_front(a, rows):
    return jnp.pad(a, ((0, 0), (rows - a.shape[1], 0), (0, 0)))


def _trunk(x, p, cache, w, rows):
    n_seq, t_len, d = x.shape
    n = n_seq * t_len
    h = x.reshape(n, d)
    depth = p.shape[0]
    outs = dict(a_conv=[], a_h=[], b_k=[], b_v=[], c_conv=[], c_C=[], c_n=[], c_m=[])
    for li in range(depth):
        j = li // 2
        final = li == depth - 1
        p_li = p[li].reshape(n, -1)
        g_mix = w["norm_mix_g"][li][None, :]
        tail_w = [w["norm_ffn_g"][li][None, :], w["ffn_w_in"][li].astype(BF16), w["ffn_w_out"][li].astype(BF16),
                  w["ple_w_gate"][li].astype(BF16), w["ple_w_proj"][li].astype(BF16)]
        if final:
            tail_w.append(w["final_g"][None, :])
        if li % 2 == 0:
            a_w = w["ab_lambda"].shape[1]
            kv_w = B_KV_HEADS * B_HEAD_DIM
            q_w = B_HEADS * B_HEAD_DIM
            keep = min(WINDOW, t_len) if cache is None else t_len
            assert keep <= rows
            state_inputs = []
            if cache is not None:
                state_inputs = [_pad_rows_front(cache["a_conv"][j], SUBLANES), cache["a_h"][j][:, None, :],
                                cache["b_k"][j].reshape(n_seq, -1, kv_w), cache["b_v"][j].reshape(n_seq, -1, kv_w)]
            wri = jnp.concatenate([_diag_tiles(w["ab_w_r"][j]), _diag_tiles(w["ab_w_i"][j])], axis=2).astype(BF16)
            w_out = w["ab_w_out"][j].astype(BF16)
            weights = [g_mix, w["ab_w_in"][j].astype(BF16), w["ab_conv_w"][j], w["ab_conv_b"][j][None, :], wri,
                       w["ab_b_r"][j][None, :], w["ab_b_i"][j][None, :], w["ab_lambda"][j][None, :],
                       w_out[:a_w], w_out[a_w:]] + tail_w
            tile_s = lambda wd: pltpu.VMEM((rows, wd), F32)
            scratch = [tile_s(a_w), tile_s(a_w), tile_s(q_w), tile_s(kv_w), tile_s(kv_w), tile_s(a_w), tile_s(q_w),
                       pltpu.VMEM((rows + SUBLANES, a_w), F32), pltpu.VMEM((1, a_w), F32)]
            if cache is None:
                scratch += [pltpu.VMEM((rows + WINDOW, kv_w), F32), pltpu.VMEM((rows + WINDOW, kv_w), F32)]
            h, u_last, k_last, v_last, h_last = _layer_call(
                _layer0_body, [w["ab_sinks"][j]], h, p_li, state_inputs, weights,
                [(n_seq, SUBLANES, a_w), (n_seq, keep, kv_w), (n_seq, keep, kv_w)], [(n_seq, 1, a_w)],
                scratch, n_seq, rows, f"layer{li}_{t_len}", final)
            outs["a_conv"].append(u_last[:, SUBLANES - (CONV_W - 1):])
            outs["a_h"].append(h_last[:, 0])
            outs["b_k"].append(k_last.reshape(n_seq, keep, B_KV_HEADS, B_HEAD_DIM))
            outs["b_v"].append(v_last.reshape(n_seq, keep, B_KV_HEADS, B_HEAD_DIM))
        else:
            c_w = w["c_conv_b"].shape[1]
            dh = c_w // C_HEADS
            w_in = w["c_w_in"][j]
            n_gate = w_in.shape[1] - 3 * c_w
            assert n_gate == 2 * C_HEADS == SUBLANES
            state_inputs = []
            if cache is not None:
                state_inputs = [_pad_rows_front(cache["c_conv"][j], SUBLANES), cache["c_C"][j], cache["c_n"][j],
                                jnp.broadcast_to(cache["c_m"][j][:, :, None], (n_seq, C_HEADS, LANES))]
            wqk = jnp.concatenate([_diag_tiles(w["c_w_q"][j]), _diag_tiles(w["c_w_k"][j]) * (dh ** -0.5)],
                                  axis=2).astype(BF16)
            weights = [g_mix, jnp.pad(w_in, ((0, 0), (0, LANES - n_gate))).astype(BF16),
                       w_in[:, 3 * c_w:].T.astype(BF16), w["c_conv_w"][j], w["c_conv_b"][j][None, :], wqk,
                       jnp.pad(w["c_b_gate"][j], (0, LANES - n_gate))[None, :], w["c_b_gate"][j][:, None],
                       w["c_norm_g"][j][None, :], w["c_w_out"][j].astype(BF16)] + tail_w
            tile_s = lambda wd: pltpu.VMEM((rows, wd), F32)
            scratch = [tile_s(c_w), tile_s(c_w), tile_s(c_w), tile_s(LANES), pltpu.VMEM((SUBLANES, rows), F32),
                       tile_s(c_w), pltpu.VMEM((rows + SUBLANES, c_w), F32), pltpu.VMEM((C_HEADS, dh, dh), F32),
                       pltpu.VMEM((C_HEADS, dh), F32), pltpu.VMEM((C_HEADS, LANES), F32)]
            h, u_last, c1, n1, m1 = _layer_call(
                _layer1_body, [], h, p_li, state_inputs, weights, [(n_seq, SUBLANES, c_w)],
                [(n_seq, C_HEADS, dh, dh), (n_seq, C_HEADS, dh), (n_seq, C_HEADS, LANES)],
                scratch, n_seq, rows, f"layer{li}_{t_len}", final)
            outs["c_conv"].append(u_last[:, SUBLANES - (CONV_W - 1):])
            outs["c_C"].append(c1)
            outs["c_n"].append(n1)
            outs["c_m"].append(m1[:, :, 0])
    y = h.reshape(n_seq, t_len, d)
    return (y,) + tuple(jnp.stack(outs[key]) for key in
                        ("a_conv", "a_h", "b_k", "b_v", "c_conv", "c_C", "c_n", "c_m"))


def kernel(x_prompt, x_sample, cache_a_conv, state_a_h, cache_b_k, cache_b_v, cache_c_conv, state_c_C, state_c_n, state_c_m, p_prompt, p_sample, norm_mix_g, norm_ffn_g, final_g, ab_w_in, ab_conv_w, ab_conv_b, ab_w_r, ab_b_r, ab_w_i, ab_b_i, ab_lambda, ab_sinks, ab_w_out, c_w_in, c_b_gate, c_conv_w, c_conv_b, c_w_q, c_w_k, c_norm_g, c_w_out, ffn_w_in, ffn_w_out, ple_w_proj, ple_w_gate):
    w = dict(norm_mix_g=norm_mix_g, norm_ffn_g=norm_ffn_g, final_g=final_g, ab_w_in=ab_w_in, ab_conv_w=ab_conv_w,
             ab_conv_b=ab_conv_b, ab_w_r=ab_w_r, ab_b_r=ab_b_r, ab_w_i=ab_w_i, ab_b_i=ab_b_i, ab_lambda=ab_lambda,
             ab_sinks=ab_sinks, ab_w_out=ab_w_out, c_w_in=c_w_in, c_b_gate=c_b_gate, c_conv_w=c_conv_w,
             c_conv_b=c_conv_b, c_w_q=c_w_q, c_w_k=c_w_k, c_norm_g=c_norm_g, c_w_out=c_w_out, ffn_w_in=ffn_w_in,
             ffn_w_out=ffn_w_out, ple_w_proj=ple_w_proj, ple_w_gate=ple_w_gate)
    prompt = _trunk(x_prompt, p_prompt, None, w, rows=min(256, x_prompt.shape[1]))
    cache = dict(a_conv=cache_a_conv, a_h=state_a_h, b_k=cache_b_k, b_v=cache_b_v, c_conv=cache_c_conv,
                 c_C=state_c_C, c_n=state_c_n, c_m=state_c_m)
    sample = _trunk(x_sample, p_sample, cache, w, rows=x_sample.shape[1])
    return (prompt[0], sample[0]) + prompt[1:] + sample[1:]
```

```python
import functools

import jax
import jax.numpy as jnp
from jax import lax
from jax.experimental import pallas as pl
from jax.experimental.pallas import tpu as pltpu

F32 = jnp.float32
BF16 = jnp.bfloat16

CHUNK = 64
NORM_EPS = 1e-6
NEG_INF = -1e30
CONV_W = 4
RG_C = 8.0
B_HEADS = 8
B_KV_HEADS = 2
B_HEAD_DIM = 64
B_GROUPS = B_HEADS // B_KV_HEADS
WINDOW = 128
C_HEADS = 4

LANES = 128
SUBLANES = 8
VMEM_LIMIT_BYTES = 60 * 1024 * 1024
FF_CHUNK = 256

_NT = (((1,), (1,)), ((), ()))
_TN = (((0,), (0,)), ((), ()))


def _resident(shape):
    nd = len(shape)
    return pl.BlockSpec(shape, lambda *_: (0,) * nd, pipeline_mode=pl.Buffered(1))


def _rms(x, g):
    return x * lax.rsqrt(jnp.mean(x * x, axis=-1, keepdims=True) + NORM_EPS) * g


def _log_sigmoid(x):
    return jnp.minimum(x, 0.0) - jnp.log1p(jnp.exp(-jnp.abs(x)))


def _softplus(x):
    return jnp.maximum(x, 0.0) + jnp.log1p(jnp.exp(-jnp.abs(x)))


def _split3(x):
    hi = x.astype(BF16)
    r1 = x - hi.astype(F32)
    mid = r1.astype(BF16)
    lo = (r1 - mid.astype(F32)).astype(BF16)
    return hi, mid, lo


def _mm(a, b):
    return jnp.dot(a, b, preferred_element_type=F32)


def _conv_tile(u, ext_ref, cw_ref, cb_ref, rows):
    ext_ref[SUBLANES:SUBLANES + rows, :] = u
    cw = cw_ref[...]
    out = cb_ref[...] + cw[0:1] * ext_ref[5:5 + rows, :]
    out = out + cw[1:2] * ext_ref[6:6 + rows, :]
    out = out + cw[2:3] * ext_ref[7:7 + rows, :]
    out = out + cw[3:4] * u
    ext_ref[0:SUBLANES, :] = ext_ref[rows:rows + SUBLANES, :]
    return out


def _rglru_tile(u, g, carry, ext_ref, cw_ref, cb_ref, wri_ref, br_ref, bi_ref, lam_ref, rows):
    uc = _conv_tile(u, ext_ref, cw_ref, cb_ref, rows)
    yield
    width = uc.shape[1]
    ucb = uc.astype(BF16)
    r_parts, i_parts = [], []
    for kt in range(width // LANES):
        pre = _mm(ucb[:, kt * LANES:(kt + 1) * LANES], wri_ref[kt])
        r_parts.append(pre[:, :LANES])
        i_parts.append(pre[:, LANES:])
    yield
    r = jax.nn.sigmoid(jnp.concatenate(r_parts, axis=1) + br_ref[...])
    ig = jax.nn.sigmoid(jnp.concatenate(i_parts, axis=1) + bi_ref[...])
    yield
    log_a = -RG_C * r * _softplus(-lam_ref[...])
    a = jnp.exp(log_a)
    bx = jnp.sqrt(-jnp.tanh(log_a) * (a * a + 1.0)) * (ig * uc)
    yield

    nb = rows // SUBLANES
    a3 = a.reshape(nb, SUBLANES, width)
    b3 = bx.reshape(nb, SUBLANES, width)
    row = lax.broadcasted_iota(jnp.int32, a3.shape, 1)
    for s in (1, 2, 4):
        keep = row >= s
        a_sh = pltpu.roll(a3, s, axis=1)
        b_sh = pltpu.roll(b3, s, axis=1)
        b3 = jnp.where(keep, a3 * b_sh + b3, b3)
        a3 = jnp.where(keep, a3 * a_sh, a3)
        yield
    groups = []
    for i in range(nb):
        hb = a3[i] * carry + b3[i]
        groups.append(hb)
        carry = hb[SUBLANES - 1:SUBLANES, :]
    yield
    h = jnp.concatenate(groups, axis=0)
    return h * jax.nn.gelu(g), carry


def _attn_core(q, kf, vf, sinks_ref, key_ok):
    n_q, n_k = q.shape[0], kf.shape[0]
    ri = lax.broadcasted_iota(jnp.int32, (n_q, n_k), 0)
    ci = lax.broadcasted_iota(jnp.int32, (n_q, n_k), 1)
    dist = jnp.abs(ri + WINDOW - ci).astype(F32)
    qb, kb, vb = q.astype(BF16), kf.astype(BF16), vf.astype(BF16)
    outs = []
    for h in range(B_HEADS):
        kv = h // B_GROUPS
        hs = slice(h * B_HEAD_DIM, (h + 1) * B_HEAD_DIM)
        kvs = slice(kv * B_HEAD_DIM, (kv + 1) * B_HEAD_DIM)
        bias = (-(2.0 ** (-8.0 * (h + 1) / B_HEADS))) * dist
        if key_ok is not None:
            bias = jnp.where(key_ok, bias, NEG_INF)
        s = lax.dot_general(qb[:, hs], kb[:, kvs], _NT, preferred_element_type=F32)
        s = s * (B_HEAD_DIM ** -0.5) + bias
        sink = sinks_ref[h]
        m = jnp.maximum(jnp.max(s, axis=-1, keepdims=True), sink)
        p = jnp.exp(s - m)
        den = jnp.sum(p, axis=-1, keepdims=True) + jnp.exp(sink - m)
        outs.append(_mm(p.astype(BF16), vb[:, kvs]) / den)
        if h % 2 == 1:
            yield
    return jnp.concatenate(outs, axis=1)


def _attn_band_tile(q, k, v, kf_ref, vf_ref, sinks_ref, t, rows):
    kf_ref[WINDOW:WINDOW + rows, :] = k
    vf_ref[WINDOW:WINDOW + rows, :] = v
    sc_rows = 2 * CHUNK
    n_k = sc_rows + WINDOW
    ri = lax.broadcasted_iota(jnp.int32, (sc_rows, n_k), 0)
    ci = lax.broadcasted_iota(jnp.int32, (sc_rows, n_k), 1)
    qc = ri // CHUNK
    jc = ci // CHUNK
    in_band = (jc >= qc) & (jc <= qc + WINDOW // CHUNK)
    outs = []
    for sc in range(rows // sc_rows):
        c0 = t * (rows // CHUNK) + sc * (sc_rows // CHUNK)
        key_ok = in_band & (jc >= WINDOW // CHUNK - c0)
        r0 = sc * sc_rows
        o_sc = yield from _attn_core(q[r0:r0 + sc_rows], kf_ref[r0:r0 + n_k, :], vf_ref[r0:r0 + n_k, :],
                                     sinks_ref, key_ok)
        outs.append(o_sc)
    kf_ref[0:WINDOW, :] = kf_ref[rows:rows + WINDOW, :]
    vf_ref[0:WINDOW, :] = vf_ref[rows:rows + WINDOW, :]
    return jnp.concatenate(outs, axis=0)


def _mlstm_tile(u, v, o, gcol, grow, ext_ref, state_in, state_out, cw_ref, cb_ref, wqk_ref, ng_ref, rows):
    c_in, n_in, m_in = state_in
    c_out, n_out, m_out = state_out
    uc = _conv_tile(u, ext_ref, cw_ref, cb_ref, rows)
    yield
    uc = uc * jax.nn.sigmoid(uc)
    width = uc.shape[1]
    dh = width // C_HEADS
    ucb = uc.astype(BF16)
    q_parts, k_parts = [], []
    for kt in range(width // LANES):
        qk = _mm(ucb[:, kt * LANES:(kt + 1) * LANES], wqk_ref[kt])
        q_parts.append(qk[:, :LANES])
        k_parts.append(qk[:, LANES:])
        if kt % 4 == 3:
            yield
    q = jnp.concatenate(q_parts, axis=1)
    k = jnp.concatenate(k_parts, axis=1)

    ri = lax.broadcasted_iota(jnp.int32, (rows, rows), 0)
    ci = lax.broadcasted_iota(jnp.int32, (rows, rows), 1)
    causal = ci <= ri
    tril = causal.astype(BF16)
    triu = (ri <= ci).astype(BF16)
    fc_col = sum(_mm(tril, part) for part in _split3(_log_sigmoid(gcol)))
    fc_row = sum(_mm(part, triu) for part in _split3(_log_sigmoid(grow)))
    yield

    outs = []
    for h in range(C_HEADS):
        hs = slice(h * dh, (h + 1) * dh)
        f_col = fc_col[:, C_HEADS + h:C_HEADS + h + 1]
        i_col = gcol[:, h:h + 1]
        f_row = fc_row[C_HEADS + h:C_HEADS + h + 1, :]
        i_row = grow[h:h + 1, :]
        m_prev = m_in[h:h + 1, 0:1]
        dlog = jnp.where(causal, f_col - f_row + i_row, -jnp.inf)
        inter = f_col + m_prev
        m_t = jnp.maximum(inter, jnp.max(dlog, axis=-1, keepdims=True))
        dw = jnp.exp(dlog - m_t)
        qh, kh = q[:, hs], k[:, hs]
        qb, vb = qh.astype(BF16), v[:, hs].astype(BF16)
        s = lax.dot_general(qb, kh.astype(BF16), _NT, preferred_element_type=F32) * dw
        w_prev = jnp.exp(inter - m_t)
        yield
        c_prev = c_in[h]
        n_prev = n_in[h:h + 1, :]
        num = _mm(s.astype(BF16), vb) + w_prev * _mm(qb, c_prev.astype(BF16))
        den = (jnp.sum(s, axis=-1, keepdims=True)
               + w_prev * jnp.sum(qh * n_prev, axis=-1, keepdims=True))
        hh = num / jnp.maximum(jnp.abs(den), jnp.exp(-m_t))
        yield

        f_last = f_col[rows - 1:rows, :]
        wlog = f_last - f_col + i_col
        m_new = jnp.maximum(f_last + m_prev, jnp.max(wlog, axis=0, keepdims=True))
        kw = jnp.exp(wlog - m_new) * kh
        decay = jnp.exp(f_last + m_prev - m_new)
        c_out[h] = lax.dot_general(kw.astype(BF16), vb, _TN, preferred_element_type=F32) + decay * c_prev
        n_out[h:h + 1, :] = decay * n_prev + jnp.sum(kw, axis=0, keepdims=True)
        m_out[h:h + 1, :] = jnp.broadcast_to(m_new, (1, LANES))
        yield

        hg = jax.nn.sigmoid(o[:, hs]) * hh
        mu = jnp.mean(hg, axis=-1, keepdims=True)
        var = jnp.mean(jnp.square(hg - mu), axis=-1, keepdims=True)
        outs.append((hg - mu) * lax.rsqrt(var + NORM_EPS) * ng_ref[:, hs])
        yield
    return jnp.concatenate(outs, axis=1)


def _tail_tile(x, mixes, wmix_refs, p, gffn_ref, win_ref, wout_ref, wg_ref, wp_ref, gfin_ref):
    y = None
    for m, w_ref in zip(mixes, wmix_refs):
        d = _mm(m.astype(BF16), w_ref[...])
        y = d if y is None else y + d
    h = x + y
    hn = _rms(h, gffn_ref[...]).astype(BF16)
    yield
    d_ff = wout_ref.shape[0]
    n_chunks = d_ff // FF_CHUNK

    def gate_up(c):
        lo = c * FF_CHUNK
        return _mm(hn, win_ref[:, lo:lo + FF_CHUNK]), _mm(hn, win_ref[:, d_ff + lo:d_ff + lo + FF_CHUNK])

    nxt = gate_up(0)
    acc = None
    for c in range(n_chunks):
        gate, up = nxt
        if c + 1 < n_chunks:
            nxt = gate_up(c + 1)
        yield
        act = (gate * jax.nn.sigmoid(gate) * up).astype(BF16)
        d = _mm(act, wout_ref[c * FF_CHUNK:(c + 1) * FF_CHUNK, :])
        acc = d if acc is None else acc + d
        yield
    h = h + acc
    h = h + jax.nn.sigmoid(_mm(h.astype(BF16), wg_ref[...])) * _mm(p.astype(BF16), wp_ref[...])
    return h if gfin_ref is None else _rms(h, gfin_ref[...])


def _round_robin(gens, weights=None):
    weights = weights or [1] * len(gens)
    results = [None] * len(gens)
    live = list(range(len(gens)))
    while live:
        for idx in list(live):
            for _ in range(weights[idx]):
                try:
                    next(gens[idx])
                except StopIteration as stop:
                    results[idx] = stop.value
                    live.remove(idx)
                    break
        yield
    return results


def _interleave(gens, weights=None):
    rounds = _round_robin(gens, weights)
    while True:
        try:
            next(rounds)
        except StopIteration as stop:
            return stop.value


def _stage_indices(n_tiles, nt):
    i = pl.program_id(0)
    tile_m = jnp.clip(i - 1, 0, n_tiles - 1)
    t_m = lax.rem(tile_m, nt)
    seq_end = jnp.logical_and(t_m == nt - 1, jnp.logical_and(i >= 1, i <= n_tiles))
    return i, t_m, seq_end


def _layer0_body(*refs, rows, nt, n_tiles, group, has_cache, final):
    it = iter(refs)
    sinks_ref, xp_ref, xt_ref, p_ref = next(it), next(it), next(it), next(it)
    if has_cache:
        conv0_ref, h0_ref, ck_ref, cv_ref = next(it), next(it), next(it), next(it)
    (gmix_ref, win_ref, cw_ref, cb_ref, wri_ref, br_ref, bi_ref, lam_ref,
     woa_ref, wob_ref, gffn_ref, fin_ref, fout_ref, wg_ref, wp_ref) = (next(it) for _ in range(15))
    gfin_ref = next(it) if final else None
    y_ref, ulast_ref, klast_ref, vlast_ref, hlast_ref = (next(it) for _ in range(5))
    g_s, u_s, q_s, k_s, v_s, ya_s, o_s, ext_ref, carry_ref = (next(it) for _ in range(9))
    if not has_cache:
        kf_ref, vf_ref = next(it), next(it)
    i, t_m, seq_end = _stage_indices(n_tiles, nt)

    @pl.when(i == 0)
    def _first():
        for ref in (g_s, u_s, q_s, k_s, v_s, ya_s, o_s):
            ref[...] = jnp.zeros_like(ref)

    if not has_cache:
        @pl.when(t_m == 0)
        def _init():
            ext_ref[0:SUBLANES, :] = jnp.zeros((SUBLANES, ext_ref.shape[1]), F32)
            carry_ref[...] = jnp.zeros_like(carry_ref)
            kf_ref[0:WINDOW, :] = jnp.zeros((WINDOW, kf_ref.shape[1]), F32)
            vf_ref[0:WINDOW, :] = jnp.zeros((WINDOW, vf_ref.shape[1]), F32)

    a_w = lam_ref.shape[1]
    q_w = B_HEADS * B_HEAD_DIM
    kv_w = B_KV_HEADS * B_HEAD_DIM
    keep = klast_ref.shape[1]
    rg_w = (cw_ref, cb_ref, wri_ref, br_ref, bi_ref, lam_ref)
    seq_rows = rows // group

    def mixer_stage(g, u, q, k, v):
        if has_cache:
            def one_seq(gi):
                r = slice(gi * seq_rows, (gi + 1) * seq_rows)
                ext = ext_ref.at[gi]
                ext[0:SUBLANES, :] = conv0_ref[gi]
                ya, carry = yield from _rglru_tile(u[r], g[r], h0_ref[gi], ext, *rg_w, seq_rows)
                hlast_ref[gi] = carry
                kf = jnp.concatenate([ck_ref[gi], k[r]], axis=0)
                vf = jnp.concatenate([cv_ref[gi], v[r]], axis=0)
                o = yield from _attn_core(q[r], kf, vf, sinks_ref, None)
                return ya, o

            per_seq = yield from _round_robin([one_seq(gi) for gi in range(group)])
            ya_s[...] = jnp.concatenate([ya for ya, _ in per_seq], axis=0)
            o_s[...] = jnp.concatenate([o for _, o in per_seq], axis=0)
        else:
            def rglru_part():
                ya, carry = yield from _rglru_tile(u, g, carry_ref[...], ext_ref, *rg_w, rows)
                carry_ref[...] = carry
                ya_s[...] = ya

            def attn_part():
                o_s[...] = yield from _attn_band_tile(q, k, v, kf_ref, vf_ref, sinks_ref, t_m, rows)

            yield from _round_robin([rglru_part(), attn_part()])

    def proj_stage(x):
        xn = _rms(x, gmix_ref[...]).astype(BF16)
        off = 0
        for ref, wd in ((g_s, a_w), (u_s, a_w), (q_s, q_w), (k_s, kv_w), (v_s, kv_w)):
            yield
            val = _mm(xn, win_ref[:, off:off + wd])
            ref[...] = val
            for gi in range(group):
                end = (gi + 1) * seq_rows
                if ref is u_s:
                    ulast_ref[gi] = val[end - SUBLANES:end]
                elif ref is k_s:
                    klast_ref[gi] = val[end - keep:end]
                elif ref is v_s:
                    vlast_ref[gi] = val[end - keep:end]
            off += wd

    tail = _tail_tile(xt_ref[...], (ya_s[...], o_s[...]), (woa_ref, wob_ref), p_ref[...],
                      gffn_ref, fin_ref, fout_ref, wg_ref, wp_ref, gfin_ref)
    mixer = mixer_stage(g_s[...], u_s[...], q_s[...], k_s[...], v_s[...])
    proj = proj_stage(xp_ref[...])
    y_ref[...] = _interleave([tail, mixer, proj], [3, 1, 1])[0]

    if not has_cache:
        @pl.when(seq_end)
        def _seq_end():
            hlast_ref[0] = carry_ref[...]


def _layer1_body(*refs, rows, nt, n_tiles, group, has_cache, final):
    it = iter(refs)
    xp_ref, xt_ref, p_ref = next(it), next(it), next(it)
    if has_cache:
        conv0_ref, c0_ref, n0_ref, m0_ref = next(it), next(it), next(it), next(it)
    (gmix_ref, win_ref, wgt_ref, cw_ref, cb_ref, wqk_ref, bg_ref, bgt_ref, ng_ref,
     wo_ref, gffn_ref, fin_ref, fout_ref, wg_ref, wp_ref) = (next(it) for _ in range(15))
    gfin_ref = next(it) if final else None
    y_ref, ulast_ref, c_out_ref, n_out_ref, m_out_ref = (next(it) for _ in range(5))
    u_s, v_s, o_s, gcol_s, grow_s, hn_s, ext_ref, c_s, n_s, m_s = (next(it) for _ in range(10))
    i, t_m, seq_end = _stage_indices(n_tiles, nt)

    @pl.when(i == 0)
    def _first():
        for ref in (u_s, v_s, o_s, gcol_s, grow_s, hn_s):
            ref[...] = jnp.zeros_like(ref)

    if not has_cache:
        @pl.when(t_m == 0)
        def _init():
            ext_ref[0:SUBLANES, :] = jnp.zeros((SUBLANES, ext_ref.shape[1]), F32)
            c_s[...] = jnp.zeros_like(c_s)
            n_s[...] = jnp.zeros_like(n_s)
            m_s[...] = jnp.zeros_like(m_s)

    width = ng_ref.shape[1]
    seq_rows = rows // group
    mix_w = (cw_ref, cb_ref, wqk_ref, ng_ref)

    def mixer_stage(u, v, o, gcol, grow):
        if has_cache:
            def one_seq(gi):
                r = slice(gi * seq_rows, (gi + 1) * seq_rows)
                ext = ext_ref.at[gi]
                ext[0:SUBLANES, :] = conv0_ref[gi]
                return (yield from _mlstm_tile(
                    u[r], v[r], o[r], gcol[r], grow[:, r], ext,
                    (c0_ref.at[gi], n0_ref.at[gi], m0_ref.at[gi]),
                    (c_out_ref.at[gi], n_out_ref.at[gi], m_out_ref.at[gi]), *mix_w, seq_rows))

            per_seq = yield from _round_robin([one_seq(gi) for gi in range(group)])
            hn_s[...] = jnp.concatenate(per_seq, axis=0)
        else:
            hn_s[...] = yield from _mlstm_tile(u, v, o, gcol, grow, ext_ref, (c_s, n_s, m_s), (c_s, n_s, m_s),
                                               *mix_w, rows)

    def proj_stage(x):
        xn = _rms(x, gmix_ref[...]).astype(BF16)
        for j, ref in enumerate((u_s, v_s, o_s)):
            yield
            val = _mm(xn, win_ref[:, j * width:(j + 1) * width])
            ref[...] = val
            if ref is u_s:
                for gi in range(group):
                    ulast_ref[gi] = val[(gi + 1) * seq_rows - SUBLANES:(gi + 1) * seq_rows]
        yield
        gcol_s[...] = _mm(xn, win_ref[:, 3 * width:3 * width + LANES]) + bg_ref[...]
        grow_s[...] = lax.dot_general(wgt_ref[...], xn, _NT, preferred_element_type=F32) + bgt_ref[...]

    tail = _tail_tile(xt_ref[...], (hn_s[...],), (wo_ref,), p_ref[...],
                      gffn_ref, fin_ref, fout_ref, wg_ref, wp_ref, gfin_ref)
    mixer = mixer_stage(u_s[...], v_s[...], o_s[...], gcol_s[...], grow_s[...])
    proj = proj_stage(xp_ref[...])
    y_ref[...] = _interleave([tail, mixer, proj])[0]

    if not has_cache:
        @pl.when(seq_end)
        def _seq_end():
            c_out_ref[0] = c_s[...]
            n_out_ref[0] = n_s[...]
            m_out_ref[0] = m_s[...]


def _layer_call(body, first_inputs, x, p, state_inputs, weights, proj_state_shapes, mixer_state_shapes, scratch,
                seq_len, rows, name, final):
    n, d = x.shape
    n_tiles = n // rows
    nt = max(1, seq_len // rows)
    group = max(1, rows // seq_len)
    assert n_tiles * rows == n and nt * rows == seq_len * group
    proj_tile_of = lambda i: jnp.minimum(i, n_tiles - 1)
    mixer_tile_of = lambda i: jnp.clip(i - 1, 0, n_tiles - 1)
    tail_tile_of = lambda i: jnp.maximum(i - 2, 0)
    tile = lambda w, tile_of: pl.BlockSpec((rows, w), lambda i: (tile_of(i), 0))
    per_seq = lambda shape, tile_of: pl.BlockSpec(
        (group,) + tuple(shape), lambda i: (tile_of(i) // nt,) + (0,) * len(shape))
    in_specs = ([pl.BlockSpec(memory_space=pltpu.SMEM) for _ in first_inputs]
                + [tile(d, proj_tile_of), tile(d, tail_tile_of), tile(p.shape[1], tail_tile_of)]
                + [per_seq(s.shape[1:], mixer_tile_of) for s in state_inputs]
                + [_resident(w.shape) for w in weights])
    out_specs = ([tile(d, tail_tile_of)] + [per_seq(s[1:], proj_tile_of) for s in proj_state_shapes]
                 + [per_seq(s[1:], mixer_tile_of) for s in mixer_state_shapes])
    out_shape = ([jax.ShapeDtypeStruct((n, d), F32)]
                 + [jax.ShapeDtypeStruct(s, F32) for s in list(proj_state_shapes) + list(mixer_state_shapes)])
    return pl.pallas_call(
        functools.partial(body, rows=rows, nt=nt, n_tiles=n_tiles, group=group, has_cache=bool(state_inputs),
                          final=final),
        grid=(n_tiles + 2,),
        in_specs=in_specs,
        out_specs=out_specs,
        out_shape=out_shape,
        scratch_shapes=scratch,
        compiler_params=pltpu.CompilerParams(dimension_semantics=("arbitrary",),
                                             vmem_limit_bytes=VMEM_LIMIT_BYTES),
        name=name,
    )(*first_inputs, x, x, p, *state_inputs, *weights)


def _diag_tiles(w):
    nb, bs, _ = w.shape
    per = LANES // bs
    w4 = w.reshape(nb // per, per, bs, bs)
    eye = jnp.eye(per, dtype=w.dtype)
    return jnp.einsum("kaij,ab->kaibj", w4, eye).reshape(nb // per, LANES, LANES)


def _pad_rows_front(a, rows):
    return jnp.pad(a, ((0, 0), (rows - a.shape[1], 0), (0, 0)))


def _trunk(x, p, cache, w, rows_by_layer):
    n_seq, t_len, d = x.shape
    n = n_seq * t_len
    h = x.reshape(n, d)
    depth = p.shape[0]
    outs = dict(a_conv=[], a_h=[], b_k=[], b_v=[], c_conv=[], c_C=[], c_n=[], c_m=[])
    for li in range(depth):
        j = li // 2
        final = li == depth - 1
        rows = rows_by_layer[li % 2]
        ext_shape = lambda wd: ((rows + SUBLANES, wd) if cache is None
                                else (rows // t_len, t_len + SUBLANES, wd))
        p_li = p[li].reshape(n, -1)
        g_mix = w["norm_mix_g"][li][None, :]
        tail_w = [w["norm_ffn_g"][li][None, :], w["ffn_w_in"][li].astype(BF16), w["ffn_w_out"][li].astype(BF16),
                  w["ple_w_gate"][li].astype(BF16), w["ple_w_proj"][li].astype(BF16)]
        if final:
            tail_w.append(w["final_g"][None, :])
        if li % 2 == 0:
            a_w = w["ab_lambda"].shape[1]
            kv_w = B_KV_HEADS * B_HEAD_DIM
            q_w = B_HEADS * B_HEAD_DIM
            keep = min(WINDOW, t_len) if cache is None else t_len
            assert keep <= rows and (cache is None or rows % t_len == 0)
            state_inputs = []
            if cache is not None:
                state_inputs = [_pad_rows_front(cache["a_conv"][j], SUBLANES), cache["a_h"][j][:, None, :],
                                cache["b_k"][j].reshape(n_seq, -1, kv_w), cache["b_v"][j].reshape(n_seq, -1, kv_w)]
            wri = jnp.concatenate([_diag_tiles(w["ab_w_r"][j]), _diag_tiles(w["ab_w_i"][j])], axis=2).astype(BF16)
            w_out = w["ab_w_out"][j].astype(BF16)
            weights = [g_mix, w["ab_w_in"][j].astype(BF16), w["ab_conv_w"][j], w["ab_conv_b"][j][None, :], wri,
                       w["ab_b_r"][j][None, :], w["ab_b_i"][j][None, :], w["ab_lambda"][j][None, :],
                       w_out[:a_w], w_out[a_w:]] + tail_w
            tile_s = lambda wd: pltpu.VMEM((rows, wd), F32)
            scratch = [tile_s(a_w), tile_s(a_w), tile_s(q_w), tile_s(kv_w), tile_s(kv_w), tile_s(a_w), tile_s(q_w),
                       pltpu.VMEM(ext_shape(a_w), F32), pltpu.VMEM((1, a_w), F32)]
            if cache is None:
                scratch += [pltpu.VMEM((rows + WINDOW, kv_w), F32), pltpu.VMEM((rows + WINDOW, kv_w), F32)]
            h, u_last, k_last, v_last, h_last = _layer_call(
                _layer0_body, [w["ab_sinks"][j]], h, p_li, state_inputs, weights,
                [(n_seq, SUBLANES, a_w), (n_seq, keep, kv_w), (n_seq, keep, kv_w)], [(n_seq, 1, a_w)],
                scratch, t_len, rows, f"layer{li}_{t_len}", final)
            outs["a_conv"].append(u_last[:, SUBLANES - (CONV_W - 1):])
            outs["a_h"].append(h_last[:, 0])
            outs["b_k"].append(k_last.reshape(n_seq, keep, B_KV_HEADS, B_HEAD_DIM))
            outs["b_v"].append(v_last.reshape(n_seq, keep, B_KV_HEADS, B_HEAD_DIM))
        else:
            c_w = w["c_conv_b"].shape[1]
            dh = c_w // C_HEADS
            w_in = w["c_w_in"][j]
            n_gate = w_in.shape[1] - 3 * c_w
            assert n_gate == 2 * C_HEADS == SUBLANES
            state_inputs = []
            if cache is not None:
                state_inputs = [_pad_rows_front(cache["c_conv"][j], SUBLANES), cache["c_C"][j], cache["c_n"][j],
                                jnp.broadcast_to(cache["c_m"][j][:, :, None], (n_seq, C_HEADS, LANES))]
            wqk = jnp.concatenate([_diag_tiles(w["c_w_q"][j]), _diag_tiles(w["c_w_k"][j]) * (dh ** -0.5)],
                                  axis=2).astype(BF16)
            weights = [g_mix, jnp.pad(w_in, ((0, 0), (0, LANES - n_gate))).astype(BF16),
                       w_in[:, 3 * c_w:].T.astype(BF16), w["c_conv_w"][j], w["c_conv_b"][j][None, :], wqk,
                       jnp.pad(w["c_b_gate"][j], (0, LANES - n_gate))[None, :], w["c_b_gate"][j][:, None],
                       w["c_norm_g"][j][None, :], w["c_w_out"][j].astype(BF16)] + tail_w
            tile_s = lambda wd: pltpu.VMEM((rows, wd), F32)
            scratch = [tile_s(c_w), tile_s(c_w), tile_s(c_w), tile_s(LANES), pltpu.VMEM((SUBLANES, rows), F32),
                       tile_s(c_w), pltpu.VMEM(ext_shape(c_w), F32), pltpu.VMEM((C_HEADS, dh, dh), F32),
                       pltpu.VMEM((C_HEADS, dh), F32), pltpu.VMEM((C_HEADS, LANES), F32)]
            h, u_last, c1, n1, m1 = _layer_call(
                _layer1_body, [], h, p_li, state_inputs, weights, [(n_seq, SUBLANES, c_w)],
                [(n_seq, C_HEADS, dh, dh), (n_seq, C_HEADS, dh), (n_seq, C_HEADS, LANES)],
                scratch, t_len, rows, f"layer{li}_{t_len}", final)
            outs["c_conv"].append(u_last[:, SUBLANES - (CONV_W - 1):])
            outs["c_C"].append(c1)
            outs["c_n"].append(n1)
            outs["c_m"].append(m1[:, :, 0])
    y = h.reshape(n_seq, t_len, d)
    return (y,) + tuple(jnp.stack(outs[key]) for key in
                        ("a_conv", "a_h", "b_k", "b_v", "c_conv", "c_C", "c_n", "c_m"))


def kernel(x_prompt, x_sample, cache_a_conv, state_a_h, cache_b_k, cache_b_v, cache_c_conv, state_c_C, state_c_n, state_c_m, p_prompt, p_sample, norm_mix_g, norm_ffn_g, final_g, ab_w_in, ab_conv_w, ab_conv_b, ab_w_r, ab_b_r, ab_w_i, ab_b_i, ab_lambda, ab_sinks, ab_w_out, c_w_in, c_b_gate, c_conv_w, c_conv_b, c_w_q, c_w_k, c_norm_g, c_w_out, ffn_w_in, ffn_w_out, ple_w_proj, ple_w_gate):
    w = dict(norm_mix_g=norm_mix_g, norm_ffn_g=norm_ffn_g, final_g=final_g, ab_w_in=ab_w_in, ab_conv_w=ab_conv_w,
             ab_conv_b=ab_conv_b, ab_w_r=ab_w_r, ab_b_r=ab_b_r, ab_w_i=ab_w_i, ab_b_i=ab_b_i, ab_lambda=ab_lambda,
             ab_sinks=ab_sinks, ab_w_out=ab_w_out, c_w_in=c_w_in, c_b_gate=c_b_gate, c_conv_w=c_conv_w,
             c_conv_b=c_conv_b, c_w_q=c_w_q, c_w_k=c_w_k, c_norm_g=c_norm_g, c_w_out=c_w_out, ffn_w_in=ffn_w_in,
             ffn_w_out=ffn_w_out, ple_w_proj=ple_w_proj, ple_w_gate=ple_w_gate)
    rows_p = min(256, x_prompt.shape[1])
    prompt = _trunk(x_prompt, p_prompt, None, w, (rows_p, rows_p))
    cache = dict(a_conv=cache_a_conv, a_h=state_a_h, b_k=cache_b_k, b_v=cache_b_v, c_conv=cache_c_conv,
                 c_C=state_c_C, c_n=state_c_n, c_m=state_c_m)
    t_s, n_s = x_sample.shape[1], x_sample.shape[0]
    sample = _trunk(x_sample, p_sample, cache, w, (t_s * min(8, n_s), t_s * min(4, n_s)))
    return (prompt[0], sample[0]) + prompt[1:] + sample[1:]
```

```python
import functools

import jax
import jax.numpy as jnp
from jax import lax
from jax.experimental import pallas as pl
from jax.experimental.pallas import tpu as pltpu

F32 = jnp.float32
BF16 = jnp.bfloat16

CHUNK = 64
NORM_EPS = 1e-6
NEG_INF = -1e30
CONV_W = 4
RG_C = 8.0
B_HEADS = 8
B_KV_HEADS = 2
B_HEAD_DIM = 64
B_GROUPS = B_HEADS // B_KV_HEADS
WINDOW = 128
C_HEADS = 4

LANES = 128
SUBLANES = 8
VMEM_LIMIT_BYTES = 60 * 1024 * 1024
FF_CHUNK = 256

_NT = (((1,), (1,)), ((), ()))
_TN = (((0,), (0,)), ((), ()))


def _resident(shape):
    nd = len(shape)
    return pl.BlockSpec(shape, lambda *_: (0,) * nd, pipeline_mode=pl.Buffered(1))


def _resident_slice(shape, index):
    nd = len(shape)
    return pl.BlockSpec((None,) + tuple(shape[1:]), lambda *_: (index,) + (0,) * (nd - 1),
                        pipeline_mode=pl.Buffered(1))


def _rms(x, g):
    return x * lax.rsqrt(jnp.mean(x * x, axis=-1, keepdims=True) + NORM_EPS) * g


def _log_sigmoid(x):
    return jnp.minimum(x, 0.0) - jnp.log1p(jnp.exp(-jnp.abs(x)))


def _softplus(x):
    return jnp.maximum(x, 0.0) + jnp.log1p(jnp.exp(-jnp.abs(x)))


def _split3(x):
    hi = x.astype(BF16)
    r1 = x - hi.astype(F32)
    mid = r1.astype(BF16)
    lo = (r1 - mid.astype(F32)).astype(BF16)
    return hi, mid, lo


def _mm(a, b):
    return jnp.dot(a, b, preferred_element_type=F32)


def _conv_tile(u, ext_ref, cw_ref, cb_ref, rows):
    ext_ref[SUBLANES:SUBLANES + rows, :] = u
    cw = cw_ref[...]
    out = cb_ref[...] + cw[0:1] * ext_ref[5:5 + rows, :]
    out = out + cw[1:2] * ext_ref[6:6 + rows, :]
    out = out + cw[2:3] * ext_ref[7:7 + rows, :]
    out = out + cw[3:4] * u
    ext_ref[0:SUBLANES, :] = ext_ref[rows:rows + SUBLANES, :]
    return out


def _rglru_tile(u, g, carry, ext_ref, cw_ref, cb_ref, wri_ref, br_ref, bi_ref, lam_ref, rows):
    uc = _conv_tile(u, ext_ref, cw_ref, cb_ref, rows)
    yield
    width = uc.shape[1]
    ucb = uc.astype(BF16)
    r_parts, i_parts = [], []
    for kt in range(width // LANES):
        pre = _mm(ucb[:, kt * LANES:(kt + 1) * LANES], wri_ref[kt])
        r_parts.append(pre[:, :LANES])
        i_parts.append(pre[:, LANES:])
    yield
    r = jax.nn.sigmoid(jnp.concatenate(r_parts, axis=1) + br_ref[...])
    ig = jax.nn.sigmoid(jnp.concatenate(i_parts, axis=1) + bi_ref[...])
    yield
    log_a = -RG_C * r * _softplus(-lam_ref[...])
    a = jnp.exp(log_a)
    bx = jnp.sqrt(-jnp.tanh(log_a) * (a * a + 1.0)) * (ig * uc)
    yield

    nb = rows // SUBLANES
    a3 = a.reshape(nb, SUBLANES, width)
    b3 = bx.reshape(nb, SUBLANES, width)
    row = lax.broadcasted_iota(jnp.int32, a3.shape, 1)
    for s in (1, 2, 4):
        keep = row >= s
        a_sh = pltpu.roll(a3, s, axis=1)
        b_sh = pltpu.roll(b3, s, axis=1)
        b3 = jnp.where(keep, a3 * b_sh + b3, b3)
        a3 = jnp.where(keep, a3 * a_sh, a3)
        yield
    groups = []
    for i in range(nb):
        hb = a3[i] * carry + b3[i]
        groups.append(hb)
        carry = hb[SUBLANES - 1:SUBLANES, :]
    yield
    h = jnp.concatenate(groups, axis=0)
    return h * jax.nn.gelu(g), carry


def _attn_problems(problems, sinks_ref):
    assert 2 * B_HEAD_DIM == LANES and B_KV_HEADS == 2 and B_GROUPS == 4
    n_q, n_k = problems[0][0].shape[0], problems[0][1].shape[0]
    ri = lax.broadcasted_iota(jnp.int32, (n_q, n_k), 0)
    ci = lax.broadcasted_iota(jnp.int32, (n_q, n_k), 1)
    dist = jnp.abs(ri + WINDOW - ci).astype(F32)
    low_half = lax.broadcasted_iota(jnp.int32, (1, LANES), 1) < B_HEAD_DIM
    swap = lambda a: pltpu.roll(a, B_HEAD_DIM, axis=1)
    slabs_out = [[None] * (B_HEADS // 2) for _ in problems]
    for kv in range(B_KV_HEADS):
        heads = range(kv * B_GROUPS, (kv + 1) * B_GROUPS)
        own_half = low_half if kv == 0 else jnp.logical_not(low_half)
        scores = []
        for q, kf, _, key_ok in problems:
            slabs = [q[:, (kv * 2 + j) * LANES:(kv * 2 + j + 1) * LANES] for j in range(2)]
            q4 = jnp.concatenate([slabs[0], swap(slabs[0]), slabs[1], swap(slabs[1])], axis=0).astype(BF16)
            k_own = jnp.where(low_half, kf if kv == 0 else swap(kf), 0.0).astype(BF16)
            s = lax.dot_general(q4, k_own, _NT, preferred_element_type=F32)
            bias = []
            for h in heads:
                b = (-(2.0 ** (-8.0 * (h + 1) / B_HEADS))) * dist
                bias.append(b if key_ok is None else jnp.where(key_ok, b, NEG_INF))
            scores.append(s * (B_HEAD_DIM ** -0.5) + jnp.concatenate(bias, axis=0))
            yield
        s_all = jnp.concatenate(scores, axis=0)
        sink = jnp.concatenate([jnp.full((n_q, 1), sinks_ref[h], F32) for h in heads] * len(problems), axis=0)
        m = jnp.maximum(jnp.max(s_all, axis=-1, keepdims=True), sink)
        yield
        p = jnp.exp(s_all - m)
        inv_den = 1.0 / (jnp.sum(p, axis=-1, keepdims=True) + jnp.exp(sink - m))
        pb = p.astype(BF16)
        yield
        for idx, (_, _, vf, _) in enumerate(problems):
            r0 = idx * B_GROUPS * n_q
            v_own = jnp.where(own_half, vf, 0.0).astype(BF16)
            o4 = _mm(pb[r0:r0 + B_GROUPS * n_q], v_own) * inv_den[r0:r0 + B_GROUPS * n_q]
            for j in range(2):
                even, odd = o4[2 * j * n_q:(2 * j + 1) * n_q], o4[(2 * j + 1) * n_q:(2 * j + 2) * n_q]
                slab = jnp.where(low_half, even, swap(odd)) if kv == 0 else jnp.where(low_half, swap(even), odd)
                slabs_out[idx][kv * 2 + j] = slab
            yield
    return [jnp.concatenate(slabs, axis=1) for slabs in slabs_out]


def _attn_band_tile(q, k, v, kf_ref, vf_ref, sinks_ref, t, rows):
    kf_ref[WINDOW:WINDOW + rows, :] = k
    vf_ref[WINDOW:WINDOW + rows, :] = v
    sc_rows = 2 * CHUNK
    n_k = sc_rows + WINDOW
    ri = lax.broadcasted_iota(jnp.int32, (sc_rows, n_k), 0)
    ci = lax.broadcasted_iota(jnp.int32, (sc_rows, n_k), 1)
    qc = ri // CHUNK
    jc = ci // CHUNK
    in_band = (jc >= qc) & (jc <= qc + WINDOW // CHUNK)
    problems = []
    for sc in range(rows // sc_rows):
        c0 = t * (rows // CHUNK) + sc * (sc_rows // CHUNK)
        key_ok = in_band & (jc >= WINDOW // CHUNK - c0)
        r0 = sc * sc_rows
        problems.append((q[r0:r0 + sc_rows], kf_ref[r0:r0 + n_k, :], vf_ref[r0:r0 + n_k, :], key_ok))
    outs = yield from _attn_problems(problems, sinks_ref)
    kf_ref[0:WINDOW, :] = kf_ref[rows:rows + WINDOW, :]
    vf_ref[0:WINDOW, :] = vf_ref[rows:rows + WINDOW, :]
    return jnp.concatenate(outs, axis=0)


def _mlstm_tile(u, v, o, gcol, grow, ext_ref, state_in, state_out, cw_ref, cb_ref, wqk_ref, ng_ref, rows):
    c_in, n_in, m_in = state_in
    c_out, n_out, m_out = state_out
    uc = _conv_tile(u, ext_ref, cw_ref, cb_ref, rows)
    yield
    uc = uc * jax.nn.sigmoid(uc)
    width = uc.shape[1]
    dh = width // C_HEADS
    ucb = uc.astype(BF16)
    q_parts, k_parts = [], []
    for kt in range(width // LANES):
        qk = _mm(ucb[:, kt * LANES:(kt + 1) * LANES], wqk_ref[kt])
        q_parts.append(qk[:, :LANES])
        k_parts.append(qk[:, LANES:])
        if kt % 4 == 3:
            yield
    q = jnp.concatenate(q_parts, axis=1)
    k = jnp.concatenate(k_parts, axis=1)

    ri = lax.broadcasted_iota(jnp.int32, (rows, rows), 0)
    ci = lax.broadcasted_iota(jnp.int32, (rows, rows), 1)
    causal = ci <= ri
    tril = causal.astype(BF16)
    triu = (ri <= ci).astype(BF16)
    fc_col = sum(_mm(tril, part) for part in _split3(_log_sigmoid(gcol)))
    fc_row = sum(_mm(part, triu) for part in _split3(_log_sigmoid(grow)))
    yield

    outs = []
    for h in range(C_HEADS):
        hs = slice(h * dh, (h + 1) * dh)
        f_col = fc_col[:, C_HEADS + h:C_HEADS + h + 1]
        i_col = gcol[:, h:h + 1]
        f_row = fc_row[C_HEADS + h:C_HEADS + h + 1, :]
        i_row = grow[h:h + 1, :]
        m_prev = m_in[h:h + 1, 0:1]
        dlog = jnp.where(causal, f_col - f_row + i_row, -jnp.inf)
        inter = f_col + m_prev
        m_t = jnp.maximum(inter, jnp.max(dlog, axis=-1, keepdims=True))
        dw = jnp.exp(dlog - m_t)
        qh, kh = q[:, hs], k[:, hs]
        qb, vb = qh.astype(BF16), v[:, hs].astype(BF16)
        s = lax.dot_general(qb, kh.astype(BF16), _NT, preferred_element_type=F32) * dw
        w_prev = jnp.exp(inter - m_t)
        yield
        c_prev = c_in[h]
        n_prev = n_in[h:h + 1, :]
        num = _mm(s.astype(BF16), vb) + w_prev * _mm(qb, c_prev.astype(BF16))
        den = (jnp.sum(s, axis=-1, keepdims=True)
               + w_prev * jnp.sum(qh * n_prev, axis=-1, keepdims=True))
        hh = num / jnp.maximum(jnp.abs(den), jnp.exp(-m_t))
        yield

        f_last = f_col[rows - 1:rows, :]
        wlog = f_last - f_col + i_col
        m_new = jnp.maximum(f_last + m_prev, jnp.max(wlog, axis=0, keepdims=True))
        kw = jnp.exp(wlog - m_new) * kh
        decay = jnp.exp(f_last + m_prev - m_new)
        c_out[h] = lax.dot_general(kw.astype(BF16), vb, _TN, preferred_element_type=F32) + decay * c_prev
        n_out[h:h + 1, :] = decay * n_prev + jnp.sum(kw, axis=0, keepdims=True)
        m_out[h:h + 1, :] = jnp.broadcast_to(m_new, (1, LANES))
        yield

        hg = jax.nn.sigmoid(o[:, hs]) * hh
        mu = jnp.mean(hg, axis=-1, keepdims=True)
        var = jnp.mean(jnp.square(hg - mu), axis=-1, keepdims=True)
        outs.append((hg - mu) * lax.rsqrt(var + NORM_EPS) * ng_ref[:, hs])
        yield
    return jnp.concatenate(outs, axis=1)


def _residual_norm(x, mixes, wmix_refs, gffn_ref):
    y = None
    for m, w_ref in zip(mixes, wmix_refs):
        d = _mm(m.astype(BF16), w_ref[...])
        y = d if y is None else y + d
    h = x + y
    return h, _rms(h, gffn_ref[...]).astype(BF16)


def _tail_tile(h, hn, p, win_ref, wout_ref, wg_ref, wp_ref, gfin_ref):
    d_ff = wout_ref.shape[0]
    n_chunks = d_ff // FF_CHUNK

    def gate_up(c):
        lo = c * FF_CHUNK
        return _mm(hn, win_ref[:, lo:lo + FF_CHUNK]), _mm(hn, win_ref[:, d_ff + lo:d_ff + lo + FF_CHUNK])

    nxt = gate_up(0)
    acc = None
    for c in range(n_chunks):
        gate, up = nxt
        if c + 1 < n_chunks:
            nxt = gate_up(c + 1)
        yield
        act = (gate * jax.nn.sigmoid(gate) * up).astype(BF16)
        d = _mm(act, wout_ref[c * FF_CHUNK:(c + 1) * FF_CHUNK, :])
        acc = d if acc is None else acc + d
        yield
    h = h + acc
    h = h + jax.nn.sigmoid(_mm(h.astype(BF16), wg_ref[...])) * _mm(p.astype(BF16), wp_ref[...])
    return h if gfin_ref is None else _rms(h, gfin_ref[...])


def _round_robin(gens, weights=None):
    weights = weights or [1] * len(gens)
    results = [None] * len(gens)
    live = list(range(len(gens)))
    while live:
        for idx in list(live):
            for _ in range(weights[idx]):
                try:
                    next(gens[idx])
                except StopIteration as stop:
                    results[idx] = stop.value
                    live.remove(idx)
                    break
        yield
    return results


def _interleave(gens, weights=None):
    rounds = _round_robin(gens, weights)
    while True:
        try:
            next(rounds)
        except StopIteration as stop:
            return stop.value


def _stage_indices(n_tiles, nt):
    i = pl.program_id(0)
    tile_m = jnp.clip(i - 1, 0, n_tiles - 1)
    t_m = lax.rem(tile_m, nt)
    seq_end = jnp.logical_and(t_m == nt - 1, jnp.logical_and(i >= 1, i <= n_tiles))
    return i, t_m, seq_end


def _layer0_body(*refs, rows, nt, n_tiles, group, has_cache, final):
    it = iter(refs)
    sinks_ref, xp_ref, xm_ref, p_ref = next(it), next(it), next(it), next(it)
    if has_cache:
        conv0_ref, h0_ref, ck_ref, cv_ref = next(it), next(it), next(it), next(it)
    (gmix_ref, win_ref, cw_ref, cb_ref, wri_ref, br_ref, bi_ref, lam_ref,
     woa_ref, wob_ref, gffn_ref, fin_ref, fout_ref, wg_ref, wp_ref) = (next(it) for _ in range(15))
    gfin_ref = next(it) if final else None
    y_ref, ulast_ref, klast_ref, vlast_ref, hlast_ref = (next(it) for _ in range(5))
    g_s, u_s, q_s, k_s, v_s, h_s, hn_s, ext_ref, carry_ref = (next(it) for _ in range(9))
    if not has_cache:
        kf_ref, vf_ref = next(it), next(it)
    i, t_m, seq_end = _stage_indices(n_tiles, nt)

    @pl.when(i == 0)
    def _first():
        for ref in (g_s, u_s, q_s, k_s, v_s, h_s, hn_s):
            ref[...] = jnp.zeros_like(ref)

    if not has_cache:
        @pl.when(t_m == 0)
        def _init():
            ext_ref[0:SUBLANES, :] = jnp.zeros((SUBLANES, ext_ref.shape[1]), F32)
            carry_ref[...] = jnp.zeros_like(carry_ref)
            kf_ref[0:WINDOW, :] = jnp.zeros((WINDOW, kf_ref.shape[1]), F32)
            vf_ref[0:WINDOW, :] = jnp.zeros((WINDOW, vf_ref.shape[1]), F32)

    a_w = lam_ref.shape[1]
    q_w = B_HEADS * B_HEAD_DIM
    kv_w = B_KV_HEADS * B_HEAD_DIM
    keep = klast_ref.shape[1]
    rg_w = (cw_ref, cb_ref, wri_ref, br_ref, bi_ref, lam_ref)
    seq_rows = rows // group

    def mixer_stage(x, g, u, q, k, v):
        if has_cache:
            def one_seq(gi):
                r = slice(gi * seq_rows, (gi + 1) * seq_rows)
                ext = ext_ref.at[gi]
                ext[0:SUBLANES, :] = conv0_ref[gi]
                ya, carry = yield from _rglru_tile(u[r], g[r], h0_ref[gi], ext, *rg_w, seq_rows)
                hlast_ref[gi] = carry
                return ya

            def all_attn():
                problems = []
                for gi in range(group):
                    r = slice(gi * seq_rows, (gi + 1) * seq_rows)
                    problems.append((q[r], jnp.concatenate([ck_ref[gi], k[r]], axis=0),
                                     jnp.concatenate([cv_ref[gi], v[r]], axis=0), None))
                return (yield from _attn_problems(problems, sinks_ref))

            *yas, outs = yield from _round_robin([one_seq(gi) for gi in range(group)] + [all_attn()],
                                                 [1] * group + [4])
            ya = jnp.concatenate(yas, axis=0)
            o = jnp.concatenate(outs, axis=0)
        else:
            def rglru_part():
                ya, carry = yield from _rglru_tile(u, g, carry_ref[...], ext_ref, *rg_w, rows)
                carry_ref[...] = carry
                return ya

            def attn_part():
                return (yield from _attn_band_tile(q, k, v, kf_ref, vf_ref, sinks_ref, t_m, rows))

            ya, o = yield from _round_robin([rglru_part(), attn_part()])
        h_s[...], hn_s[...] = _residual_norm(x, (ya, o), (woa_ref, wob_ref), gffn_ref)

    def proj_stage(x):
        xn = _rms(x, gmix_ref[...]).astype(BF16)
        off = 0
        for ref, wd in ((g_s, a_w), (u_s, a_w), (q_s, q_w), (k_s, kv_w), (v_s, kv_w)):
            yield
            val = _mm(xn, win_ref[:, off:off + wd])
            ref[...] = val
            for gi in range(group):
                end = (gi + 1) * seq_rows
                if ref is u_s:
                    ulast_ref[gi] = val[end - SUBLANES:end]
                elif ref is k_s:
                    klast_ref[gi] = val[end - keep:end]
                elif ref is v_s:
                    vlast_ref[gi] = val[end - keep:end]
            off += wd

    tail = _tail_tile(h_s[...], hn_s[...], p_ref[...], fin_ref, fout_ref, wg_ref, wp_ref, gfin_ref)
    mixer = mixer_stage(xm_ref[...], g_s[...], u_s[...], q_s[...], k_s[...], v_s[...])
    proj = proj_stage(xp_ref[...])
    y_ref[...] = _interleave([tail, mixer, proj], [2, 1, 1])[0]

    if not has_cache:
        @pl.when(seq_end)
        def _seq_end():
            hlast_ref[0] = carry_ref[...]


def _layer1_body(*refs, rows, nt, n_tiles, group, has_cache, final):
    it = iter(refs)
    xp_ref, xm_ref, p_ref = next(it), next(it), next(it)
    if has_cache:
        conv0_ref, c0_ref, n0_ref, m0_ref = next(it), next(it), next(it), next(it)
    (gmix_ref, win_ref, cw_ref, cb_ref, wqk_ref, bg_ref, ng_ref,
     wo_ref, gffn_ref, fin_ref, fout_ref, wg_ref, wp_ref) = (next(it) for _ in range(13))
    gfin_ref = next(it) if final else None
    y_ref, ulast_ref, c_out_ref, n_out_ref, m_out_ref = (next(it) for _ in range(5))
    u_s, v_s, o_s, gcol_s, grow_s, h_s, hn_s, ext_ref, c_s, n_s, m_s = (next(it) for _ in range(11))
    i, t_m, seq_end = _stage_indices(n_tiles, nt)

    @pl.when(i == 0)
    def _first():
        for ref in (u_s, v_s, o_s, gcol_s, grow_s, h_s, hn_s):
            ref[...] = jnp.zeros_like(ref)

    if not has_cache:
        @pl.when(t_m == 0)
        def _init():
            ext_ref[0:SUBLANES, :] = jnp.zeros((SUBLANES, ext_ref.shape[1]), F32)
            c_s[...] = jnp.zeros_like(c_s)
            n_s[...] = jnp.zeros_like(n_s)
            m_s[...] = jnp.zeros_like(m_s)

    width = ng_ref.shape[1]
    seq_rows = rows // group
    mix_w = (cw_ref, cb_ref, wqk_ref, ng_ref)

    def mixer_stage(x, u, v, o, gcol, grow):
        if has_cache:
            def one_seq(gi):
                r = slice(gi * seq_rows, (gi + 1) * seq_rows)
                ext = ext_ref.at[gi]
                ext[0:SUBLANES, :] = conv0_ref[gi]
                return (yield from _mlstm_tile(
                    u[r], v[r], o[r], gcol[r], grow[:, r], ext,
                    (c0_ref.at[gi], n0_ref.at[gi], m0_ref.at[gi]),
                    (c_out_ref.at[gi], n_out_ref.at[gi], m_out_ref.at[gi]), *mix_w, seq_rows))

            per_seq = yield from _round_robin([one_seq(gi) for gi in range(group)])
            mixed = jnp.concatenate(per_seq, axis=0)
        else:
            mixed = yield from _mlstm_tile(u, v, o, gcol, grow, ext_ref, (c_s, n_s, m_s), (c_s, n_s, m_s),
                                           *mix_w, rows)
        h_s[...], hn_s[...] = _residual_norm(x, (mixed,), (wo_ref,), gffn_ref)

    def proj_stage(x):
        xn = _rms(x, gmix_ref[...]).astype(BF16)
        for j, ref in enumerate((u_s, v_s, o_s)):
            yield
            val = _mm(xn, win_ref[:, j * width:(j + 1) * width])
            ref[...] = val
            if ref is u_s:
                for gi in range(group):
                    ulast_ref[gi] = val[(gi + 1) * seq_rows - SUBLANES:(gi + 1) * seq_rows]
        yield
        gcol = _mm(xn, win_ref[:, 3 * width:3 * width + LANES]) + bg_ref[...]
        gcol_s[...] = gcol
        grow_s[...] = gcol.T[0:SUBLANES]

    tail = _tail_tile(h_s[...], hn_s[...], p_ref[...], fin_ref, fout_ref, wg_ref, wp_ref, gfin_ref)
    mixer = mixer_stage(xm_ref[...], u_s[...], v_s[...], o_s[...], gcol_s[...], grow_s[...])
    proj = proj_stage(xp_ref[...])
    y_ref[...] = _interleave([tail, mixer, proj])[0]

    if not has_cache:
        @pl.when(seq_end)
        def _seq_end():
            c_out_ref[0] = c_s[...]
            n_out_ref[0] = n_s[...]
            m_out_ref[0] = m_s[...]


def _layer_call(body, first_inputs, x, p, layer, state_inputs, weights, proj_state_shapes, mixer_state_shapes,
                scratch, seq_len, rows, name, final):
    n, d = x.shape
    n_tiles = n // rows
    nt = max(1, seq_len // rows)
    group = max(1, rows // seq_len)
    assert n_tiles * rows == n and nt * rows == seq_len * group
    proj_tile_of = lambda i: jnp.minimum(i, n_tiles - 1)
    mixer_tile_of = lambda i: jnp.clip(i - 1, 0, n_tiles - 1)
    tail_tile_of = lambda i: jnp.maximum(i - 2, 0)
    tile = lambda w, tile_of: pl.BlockSpec((rows, w), lambda i: (tile_of(i), 0))
    per_seq = lambda shape, tile_of: pl.BlockSpec(
        (group,) + tuple(shape), lambda i: (tile_of(i) // nt,) + (0,) * len(shape))
    in_specs = ([pl.BlockSpec(memory_space=pltpu.SMEM) for _ in first_inputs]
                + [tile(d, proj_tile_of), tile(d, mixer_tile_of),
                   pl.BlockSpec((None, rows, p.shape[2]), lambda i: (layer, tail_tile_of(i), 0))]
                + [per_seq(s.shape[1:], mixer_tile_of) for s in state_inputs]
                + [_resident(w.shape) if not isinstance(w, tuple) else _resident_slice(w[0].shape, w[1])
                   for w in weights])
    out_specs = ([tile(d, tail_tile_of)] + [per_seq(s[1:], proj_tile_of) for s in proj_state_shapes]
                 + [per_seq(s[1:], mixer_tile_of) for s in mixer_state_shapes])
    out_shape = ([jax.ShapeDtypeStruct((n, d), F32)]
                 + [jax.ShapeDtypeStruct(s, F32) for s in list(proj_state_shapes) + list(mixer_state_shapes)])
    return pl.pallas_call(
        functools.partial(body, rows=rows, nt=nt, n_tiles=n_tiles, group=group, has_cache=bool(state_inputs),
                          final=final),
        grid=(n_tiles + 2,),
        in_specs=in_specs,
        out_specs=out_specs,
        out_shape=out_shape,
        scratch_shapes=scratch,
        compiler_params=pltpu.CompilerParams(dimension_semantics=("arbitrary",),
                                             vmem_limit_bytes=VMEM_LIMIT_BYTES),
        name=name,
    )(*first_inputs, x, x, p, *state_inputs, *[w[0] if isinstance(w, tuple) else w for w in weights])


def _diag_tiles(w):
    nb, bs, _ = w.shape
    per = LANES // bs
    w4 = w.reshape(nb // per, per, bs, bs)
    eye = jnp.eye(per, dtype=w.dtype)
    return jnp.einsum("kaij,ab->kaibj", w4, eye).reshape(nb // per, LANES, LANES)


def _pad_rows_front(a, rows):
    return jnp.pad(a, ((0, 0), (rows - a.shape[1], 0), (0, 0)))


def _trunk(x, p, cache, w, rows_by_layer):
    n_seq, t_len, d = x.shape
    n = n_seq * t_len
    h = x.reshape(n, d)
    depth = p.shape[0]
    p = p.reshape(depth, n, -1)
    stacked_bf16 = {key: w[key].astype(BF16) for key in ("ffn_w_in", "ffn_w_out", "ple_w_gate", "ple_w_proj")}
    outs = dict(a_conv=[], a_h=[], b_k=[], b_v=[], c_conv=[], c_C=[], c_n=[], c_m=[])
    for li in range(depth):
        j = li // 2
        final = li == depth - 1
        rows = rows_by_layer[li % 2]
        ext_shape = lambda wd: ((rows + SUBLANES, wd) if cache is None
                                else (rows // t_len, t_len + SUBLANES, wd))
        g_mix = w["norm_mix_g"][li][None, :]
        tail_w = [w["norm_ffn_g"][li][None, :]] + [
            (stacked_bf16[key], li) for key in ("ffn_w_in", "ffn_w_out", "ple_w_gate", "ple_w_proj")]
        if final:
            tail_w.append(w["final_g"][None, :])
        if li % 2 == 0:
            a_w = w["ab_lambda"].shape[1]
            kv_w = B_KV_HEADS * B_HEAD_DIM
            q_w = B_HEADS * B_HEAD_DIM
            keep = min(WINDOW, t_len) if cache is None else t_len
            assert keep <= rows and (cache is None or rows % t_len == 0)
            state_inputs = []
            if cache is not None:
                state_inputs = [_pad_rows_front(cache["a_conv"][j], SUBLANES), cache["a_h"][j][:, None, :],
                                cache["b_k"][j].reshape(n_seq, -1, kv_w), cache["b_v"][j].reshape(n_seq, -1, kv_w)]
            wri = jnp.concatenate([_diag_tiles(w["ab_w_r"][j]), _diag_tiles(w["ab_w_i"][j])], axis=2).astype(BF16)
            w_out = w["ab_w_out"][j].astype(BF16)
            weights = [g_mix, w["ab_w_in"][j].astype(BF16), w["ab_conv_w"][j], w["ab_conv_b"][j][None, :], wri,
                       w["ab_b_r"][j][None, :], w["ab_b_i"][j][None, :], w["ab_lambda"][j][None, :],
                       w_out[:a_w], w_out[a_w:]] + tail_w
            tile_s = lambda wd: pltpu.VMEM((rows, wd), F32)
            scratch = [tile_s(a_w), tile_s(a_w), tile_s(q_w), tile_s(kv_w), tile_s(kv_w), tile_s(d),
                       pltpu.VMEM((rows, d), BF16),
                       pltpu.VMEM(ext_shape(a_w), F32), pltpu.VMEM((1, a_w), F32)]
            if cache is None:
                scratch += [pltpu.VMEM((rows + WINDOW, kv_w), F32), pltpu.VMEM((rows + WINDOW, kv_w), F32)]
            h, u_last, k_last, v_last, h_last = _layer_call(
                _layer0_body, [w["ab_sinks"][j]], h, p, li, state_inputs, weights,
                [(n_seq, SUBLANES, a_w), (n_seq, keep, kv_w), (n_seq, keep, kv_w)], [(n_seq, 1, a_w)],
                scratch, t_len, rows, f"layer{li}_{t_len}", final)
            outs["a_conv"].append(u_last[:, SUBLANES - (CONV_W - 1):])
            outs["a_h"].append(h_last[:, 0])
            outs["b_k"].append(k_last.reshape(n_seq, keep, B_KV_HEADS, B_HEAD_DIM))
            outs["b_v"].append(v_last.reshape(n_seq, keep, B_KV_HEADS, B_HEAD_DIM))
        else:
            c_w = w["c_conv_b"].shape[1]
            dh = c_w // C_HEADS
            w_in = w["c_w_in"][j]
            n_gate = w_in.shape[1] - 3 * c_w
            assert n_gate == 2 * C_HEADS == SUBLANES
            state_inputs = []
            if cache is not None:
                state_inputs = [_pad_rows_front(cache["c_conv"][j], SUBLANES), cache["c_C"][j], cache["c_n"][j],
                                jnp.broadcast_to(cache["c_m"][j][:, :, None], (n_seq, C_HEADS, LANES))]
            wqk = jnp.concatenate([_diag_tiles(w["c_w_q"][j]), _diag_tiles(w["c_w_k"][j]) * (dh ** -0.5)],
                                  axis=2).astype(BF16)
            weights = [g_mix, jnp.pad(w_in, ((0, 0), (0, LANES - n_gate))).astype(BF16),
                       w["c_conv_w"][j], w["c_conv_b"][j][None, :], wqk,
                       jnp.pad(w["c_b_gate"][j], (0, LANES - n_gate))[None, :],
                       w["c_norm_g"][j][None, :], w["c_w_out"][j].astype(BF16)] + tail_w
            tile_s = lambda wd: pltpu.VMEM((rows, wd), F32)
            scratch = [tile_s(c_w), tile_s(c_w), tile_s(c_w), tile_s(LANES), pltpu.VMEM((SUBLANES, rows), F32),
                       tile_s(d), pltpu.VMEM((rows, d), BF16),
                       pltpu.VMEM(ext_shape(c_w), F32), pltpu.VMEM((C_HEADS, dh, dh), F32),
                       pltpu.VMEM((C_HEADS, dh), F32), pltpu.VMEM((C_HEADS, LANES), F32)]
            h, u_last, c1, n1, m1 = _layer_call(
                _layer1_body, [], h, p, li, state_inputs, weights, [(n_seq, SUBLANES, c_w)],
                [(n_seq, C_HEADS, dh, dh), (n_seq, C_HEADS, dh), (n_seq, C_HEADS, LANES)],
                scratch, t_len, rows, f"layer{li}_{t_len}", final)
            outs["c_conv"].append(u_last[:, SUBLANES - (CONV_W - 1):])
            outs["c_C"].append(c1)
            outs["c_n"].append(n1)
            outs["c_m"].append(m1[:, :, 0])
    y = h.reshape(n_seq, t_len, d)
    return (y,) + tuple(jnp.stack(outs[key]) for key in
                        ("a_conv", "a_h", "b_k", "b_v", "c_conv", "c_C", "c_n", "c_m"))


def kernel(x_prompt, x_sample, cache_a_conv, state_a_h, cache_b_k, cache_b_v, cache_c_conv, state_c_C, state_c_n, state_c_m, p_prompt, p_sample, norm_mix_g, norm_ffn_g, final_g, ab_w_in, ab_conv_w, ab_conv_b, ab_w_r, ab_b_r, ab_w_i, ab_b_i, ab_lambda, ab_sinks, ab_w_out, c_w_in, c_b_gate, c_conv_w, c_conv_b, c_w_q, c_w_k, c_norm_g, c_w_out, ffn_w_in, ffn_w_out, ple_w_proj, ple_w_gate):
    w = dict(norm_mix_g=norm_mix_g, norm_ffn_g=norm_ffn_g, final_g=final_g, ab_w_in=ab_w_in, ab_conv_w=ab_conv_w,
             ab_conv_b=ab_conv_b, ab_w_r=ab_w_r, ab_b_r=ab_b_r, ab_w_i=ab_w_i, ab_b_i=ab_b_i, ab_lambda=ab_lambda,
             ab_sinks=ab_sinks, ab_w_out=ab_w_out, c_w_in=c_w_in, c_b_gate=c_b_gate, c_conv_w=c_conv_w,
             c_conv_b=c_conv_b, c_w_q=c_w_q, c_w_k=c_w_k, c_norm_g=c_norm_g, c_w_out=c_w_out, ffn_w_in=ffn_w_in,
             ffn_w_out=ffn_w_out, ple_w_proj=ple_w_proj, ple_w_gate=ple_w_gate)
    rows_p = min(256, x_prompt.shape[1])
    prompt = _trunk(x_prompt, p_prompt, None, w, (rows_p, rows_p))
    cache = dict(a_conv=cache_a_conv, a_h=state_a_h, b_k=cache_b_k, b_v=cache_b_v, c_conv=cache_c_conv,
                 c_C=state_c_C, c_n=state_c_n, c_m=state_c_m)
    t_s, n_s = x_sample.shape[1], x_sample.shape[0]
    sample = _trunk(x_sample, p_sample, cache, w, (t_s * min(8, n_s), t_s * min(4, n_s)))
    return (prompt[0], sample[0]) + prompt[1:] + sample[1:]
```
